```python
import jax, jax.numpy as jnp
from jax import lax
import numpy as np

D_MODEL = 1024
BATCH = 8
SEQ = 8192
DEPTH = 2

N_META = 16
NORM_EPS = 1e-6
RWKV_HEADS = 8
RWKV_HEAD_DIM = 64
RWKV_WIDTH = RWKV_HEADS * RWKV_HEAD_DIM
LORA_W = 64
LORA_A = 64
LORA_G = 128
RWKV_IN = 3 * RWKV_WIDTH + LORA_W + LORA_A + LORA_G
GN_EPS = 64e-5
LRU_BLOCKS = 8
LRU_BLOCK = 64
LRU_WIDTH = LRU_BLOCKS * LRU_BLOCK
CONV_WIDTH = 4
LRU_C = 8.0
EVEN_IN = RWKV_IN + 2 * LRU_WIDTH
EVEN_MIX = RWKV_WIDTH + LRU_WIDTH
GLA_HEADS = 4
GLA_DK = 128
GLA_DV = 256
GLA_KEY_WIDTH = GLA_HEADS * GLA_DK
GLA_VAL_WIDTH = GLA_HEADS * GLA_DV
GLA_LORA = 16
GLA_TAU = 16.0
GLA_CHUNK = 64
ODD_IN = 2 * GLA_KEY_WIDTH + 2 * GLA_VAL_WIDTH + GLA_LORA
N_GROUPS = 4
EXPERTS_PER_GROUP = 8
N_EXPERTS = N_GROUPS * EXPERTS_PER_GROUP
TOP_K = 2
EXPERT_FF = 256
N_EVEN = (DEPTH + 1) // 2
N_ODD = DEPTH // 2

kernel_name = 'hybrid_rwkv7_rglru_gla_hmoe'


def rms_norm(x, gain):
    xf = x.astype(jnp.float32)
    y = xf * lax.rsqrt(jnp.mean(xf * xf, axis=-1, keepdims=True) + NORM_EPS)
    return (y * gain.astype(jnp.float32)).astype(x.dtype)


def token_shift(p):
    return jnp.pad(p, ((0, 0), (1, 0), (0, 0)))[:, :-1]


def rwkv7_mix(p, mu, decay0, decay_up, iclr0, iclr_up, gate_up, k_k, k_a, r_k, ln_g, ln_b):
    B, T, _ = p.shape
    f32 = jnp.float32
    H, Dh, W = RWKV_HEADS, RWKV_HEAD_DIM, RWKV_WIDTH
    p = p + mu * (token_shift(p) - p)
    r, k, v, wd, ad, gd = jnp.split(p, [W, 2 * W, 3 * W, 3 * W + LORA_W, 3 * W + LORA_W + LORA_A], axis=-1)
    w = -jax.nn.softplus(-(decay0 + jnp.tanh(wd) @ decay_up).astype(f32)) - 0.5
    decay = jnp.exp(-jnp.exp(w))
    a = jax.nn.sigmoid((iclr0 + ad @ iclr_up).astype(f32))
    g = (jax.nn.sigmoid(gd) @ gate_up).astype(f32)
    heads = lambda t: t.reshape(B, T, H, Dh)
    kk = heads((k * k_k).astype(f32))
    kk = kk / jnp.maximum(jnp.linalg.norm(kk, axis=-1, keepdims=True), 1e-12)
    k_mod = k.astype(f32) * (1.0 + (a - 1.0) * k_a.astype(f32))
    r_h, k_h, v_h, a_h, w_h = heads(r.astype(f32)), heads(k_mod), heads(v.astype(f32)), heads(a), heads(decay)
    tm = lambda t: jnp.moveaxis(t, 1, 0)

    def step(S, inp):
        r_t, w_t, k_t, v_t, kk_t, b_t = inp
        sa = jnp.einsum('bhvk,bhk->bhv', S, kk_t)
        S = S * w_t[:, :, None, :] - sa[..., None] * b_t[:, :, None, :] + v_t[..., None] * k_t[:, :, None, :]
        return S, jnp.einsum('bhvk,bhk->bhv', S, r_t)

    S0 = jnp.zeros((B, H, Dh, Dh), f32)
    _, o = lax.scan(step, S0, (tm(r_h), tm(w_h), tm(k_h), tm(v_h), tm(kk), tm(kk * a_h)))
    o = jnp.moveaxis(o, 0, 1)
    mean = jnp.mean(o, axis=-1, keepdims=True)
    var = jnp.mean(jnp.square(o - mean), axis=-1, keepdims=True)
    o = ((o - mean) * lax.rsqrt(var + GN_EPS)).reshape(B, T, W) * ln_g.astype(f32) + ln_b.astype(f32)
    bonus = jnp.sum(r_h * k_h * r_k.astype(f32).reshape(H, Dh), axis=-1, keepdims=True) * v_h
    o = (o + bonus.reshape(B, T, W)) * g
    return o.astype(p.dtype)


def rglru_mix(xb, gb, conv_w, conv_b, wa, ba, wx, bx, lam):
    B, T, C = xb.shape
    f32 = jnp.float32
    xc = lax.conv_general_dilated(xb, conv_w[:, None, :], window_strides=(1,),
                                  padding=[(CONV_WIDTH - 1, 0)],
                                  dimension_numbers=('NWC', 'WIO', 'NWC'),
                                  feature_group_count=C) + conv_b
    blocks = xc.reshape(B, T, LRU_BLOCKS, LRU_BLOCK)
    gate_r = jax.nn.sigmoid((jnp.einsum('btni,nij->btnj', blocks, wa).reshape(B, T, C) + ba).astype(f32))
    gate_i = jax.nn.sigmoid((jnp.einsum('btni,nij->btnj', blocks, wx).reshape(B, T, C) + bx).astype(f32))
    log_a = -LRU_C * gate_r * jax.nn.softplus(-lam.astype(f32))
    a = jnp.exp(log_a)
    u = jnp.sqrt(-jnp.expm1(2.0 * log_a)) * gate_i * xc.astype(f32)

    def combine(left, right):
        al, bl = left
        ar, br = right
        return al * ar, ar * bl + br

    _, h = lax.associative_scan(combine, (a, u), axis=1)
    return (h * jax.nn.gelu(gb.astype(f32))).astype(xb.dtype)


def gla_chunk(state, q, k, v, log_a):
    L = q.shape[2]
    cum = jnp.cumsum(log_a, axis=2)
    causal = jnp.tril(jnp.ones((L, L), dtype=bool))
    diff = cum[:, :, :, None, :] - cum[:, :, None, :, :]
    decay = jnp.where(causal[:, :, None], jnp.exp(jnp.minimum(diff, 0.0)), 0.0)
    scores = jnp.einsum('bhtk,bhsk,bhtsk->bhts', q, k, decay)
    out = jnp.einsum('bhts,bhsv->bhtv', scores, v) + jnp.einsum('bhtk,bhkv->bhtv', q * jnp.exp(cum), state)
    last = cum[:, :, -1:, :]
    new_state = state * jnp.exp(last)[:, :, 0, :, None] + jnp.einsum('bhsk,bhsv->bhkv', k * jnp.exp(last - cum), v)
    return new_state, out


def gla_mix(p, g_up, g_b, head_gain):
    B, T, _ = p.shape
    f32 = jnp.float32
    kw, vw = GLA_KEY_WIDTH, GLA_VAL_WIDTH
    q, k, v, gd, r = jnp.split(p, [kw, 2 * kw, 2 * kw + vw, 2 * kw + vw + GLA_LORA], axis=-1)
    log_a = jax.nn.log_sigmoid((gd @ g_up + g_b).astype(f32)) / GLA_TAU
    heads = lambda t, d: jnp.transpose(t.astype(f32).reshape(B, T, GLA_HEADS, d), (0, 2, 1, 3))
    q = heads(q, GLA_DK) * (GLA_DK ** -0.5)
    k = heads(k, GLA_DK)
    v = heads(v, GLA_DV)
    log_a = heads(log_a, GLA_DK)
    state = jnp.zeros((B, GLA_HEADS, GLA_DK, GLA_DV), f32)
    state, out_meta = gla_chunk(state, q[:, :, :N_META], k[:, :, :N_META], v[:, :, :N_META], log_a[:, :, :N_META])
    n_chunks = (T - N_META) // GLA_CHUNK
    chunked = lambda t: jnp.moveaxis(t[:, :, N_META:].reshape(B, GLA_HEADS, n_chunks, GLA_CHUNK, t.shape[-1]), 2, 0)
    _, out_rest = lax.scan(lambda s, c: gla_chunk(s, *c), state, (chunked(q), chunked(k), chunked(v), chunked(log_a)))
    out_rest = jnp.moveaxis(out_rest, 0, 2).reshape(B, GLA_HEADS, T - N_META, GLA_DV)
    o = jnp.concatenate([out_meta, out_rest], axis=2).transpose(0, 2, 1, 3)
    o = o * lax.rsqrt(jnp.mean(o * o, axis=-1, keepdims=True) + NORM_EPS)
    o = o.reshape(B, T, vw) * head_gain.astype(f32) * jax.nn.silu(r.astype(f32))
    return o.astype(p.dtype)


def hier_moe(h, w_group, b_group, w_router, b_router, w_gate, w_up, w_down):
    B, T, D = h.shape
    hf = h.reshape(B * T, D)
    g_logits = (hf @ w_group + b_group).astype(jnp.float32)
    g_idx = jnp.argmax(g_logits, axis=-1)
    g_prob = jnp.take_along_axis(jax.nn.softmax(g_logits, axis=-1), g_idx[:, None], axis=-1)
    e_logits = (hf @ w_router + b_router).astype(jnp.float32).reshape(-1, N_GROUPS, EXPERTS_PER_GROUP)
    e_logits = jnp.take_along_axis(e_logits, g_idx[:, None, None], axis=1)[:, 0]
    top_val, top_idx = lax.top_k(e_logits, TOP_K)
    weights = jax.nn.softmax(top_val, axis=-1) * g_prob
    expert_idx = g_idx[:, None] * EXPERTS_PER_GROUP + top_idx
    combine = jnp.sum(jax.nn.one_hot(expert_idx, N_EXPERTS, dtype=jnp.float32) * weights[..., None], axis=1).astype(h.dtype)
    y = jnp.zeros_like(hf)
    for e in range(N_EXPERTS):
        hid = jax.nn.silu(hf @ w_gate[e]) * (hf @ w_up[e])
        y = y + (hid @ w_down[e]) * combine[:, e:e + 1]
    return y.reshape(B, T, D)


def setup_inputs(seed: int = 0) -> dict:
    key = jax.random.key(seed)
    ks = iter(jax.random.split(key, 48))
    nrm = lambda shape, scale: jax.random.normal(next(ks), shape, jnp.float32) * scale
    unif = lambda shape, lo, hi: jax.random.uniform(next(ks), shape, jnp.float32, lo, hi)
    E, O, Ld, D = N_EVEN, N_ODD, DEPTH, D_MODEL
    a_pow = unif((E, LRU_WIDTH), 0.9, 0.999) ** (1.0 / LRU_C)
    return {
        'x': nrm((BATCH, SEQ, D), 1.0),
        'meta': nrm((N_META, D), 1.0),
        'e_norm': 1.0 + nrm((E, D), 0.02),
        'e_w_in': nrm((E, D, EVEN_IN), D ** -0.5),
        'e_mu': unif((E, RWKV_IN), 0.0, 1.0),
        'e_decay0': unif((E, RWKV_WIDTH), -6.0, -1.0),
        'e_decay_up': nrm((E, LORA_W, RWKV_WIDTH), 0.1 * LORA_W ** -0.5),
        'e_iclr0': nrm((E, RWKV_WIDTH), 0.5),
        'e_iclr_up': nrm((E, LORA_A, RWKV_WIDTH), 0.1 * LORA_A ** -0.5),
        'e_gate_up': nrm((E, LORA_G, RWKV_WIDTH), LORA_G ** -0.5),
        'e_k_k': 0.85 + nrm((E, RWKV_WIDTH), 0.02),
        'e_k_a': 1.0 + nrm((E, RWKV_WIDTH), 0.02),
        'e_r_k': nrm((E, RWKV_WIDTH), 0.1),
        'e_ln_g': 1.0 + nrm((E, RWKV_WIDTH), 0.02),
        'e_ln_b': nrm((E, RWKV_WIDTH), 0.01),
        'e_conv_w': nrm((E, CONV_WIDTH, LRU_WIDTH), CONV_WIDTH ** -0.5),
        'e_conv_b': nrm((E, LRU_WIDTH), 0.01),
        'e_lru_wa': nrm((E, LRU_BLOCKS, LRU_BLOCK, LRU_BLOCK), LRU_BLOCK ** -0.5),
        'e_lru_ba': nrm((E, LRU_WIDTH), 0.01),
        'e_lru_wx': nrm((E, LRU_BLOCKS, LRU_BLOCK, LRU_BLOCK), LRU_BLOCK ** -0.5),
        'e_lru_bx': nrm((E, LRU_WIDTH), 0.01),
        'e_lru_lambda': jnp.log(a_pow) - jnp.log1p(-a_pow),
        'e_w_out': nrm((E, EVEN_MIX, D), EVEN_MIX ** -0.5),
        'o_norm': 1.0 + nrm((O, D), 0.02),
        'o_w_in': nrm((O, D, ODD_IN), D ** -0.5),
        'o_g_up': nrm((O, GLA_LORA, GLA_KEY_WIDTH), GLA_LORA ** -0.5),
        'o_g_b': nrm((O, GLA_KEY_WIDTH), 0.1),
        'o_head_gain': 1.0 + nrm((O, GLA_VAL_WIDTH), 0.02),
        'o_w_out': nrm((O, GLA_VAL_WIDTH, D), GLA_VAL_WIDTH ** -0.5),
        'm_norm': 1.0 + nrm((Ld, D), 0.02),
        'm_w_group': nrm((Ld, D, N_GROUPS), D ** -0.5),
        'm_b_group': nrm((Ld, N_GROUPS), 0.01),
        'm_w_router': nrm((Ld, D, N_EXPERTS), D ** -0.5),
        'm_b_router': nrm((Ld, N_EXPERTS), 0.01),
        'm_w_gate': nrm((Ld, N_EXPERTS, D, EXPERT_FF), D ** -0.5),
        'm_w_up': nrm((Ld, N_EXPERTS, D, EXPERT_FF), D ** -0.5),
        'm_w_down': nrm((Ld, N_EXPERTS, EXPERT_FF, D), EXPERT_FF ** -0.5),
        'final_norm': 1.0 + nrm((D,), 0.02),
    }


def reference(x, meta, e_norm, e_w_in, e_mu, e_decay0, e_decay_up, e_iclr0, e_iclr_up, e_gate_up,
              e_k_k, e_k_a, e_r_k, e_ln_g, e_ln_b, e_conv_w, e_conv_b, e_lru_wa, e_lru_ba, e_lru_wx,
              e_lru_bx, e_lru_lambda, e_w_out, o_norm, o_w_in, o_g_up, o_g_b, o_head_gain, o_w_out,
              m_norm, m_w_group, m_b_group, m_w_router, m_b_router, m_w_gate, m_w_up, m_w_down,
              final_norm):
    B = x.shape[0]
    h = jnp.concatenate([jnp.broadcast_to(meta[None].astype(x.dtype), (B, N_META, x.shape[-1])), x], axis=1)
    for layer in range(DEPTH):
        i = layer // 2
        if layer % 2 == 0:
            p = rms_norm(h, e_norm[i]) @ e_w_in[i]
            y_a = rwkv7_mix(p[..., :RWKV_IN], e_mu[i], e_decay0[i], e_decay_up[i], e_iclr0[i], e_iclr_up[i],
                            e_gate_up[i], e_k_k[i], e_k_a[i], e_r_k[i], e_ln_g[i], e_ln_b[i])
            y_b = rglru_mix(p[..., RWKV_IN:RWKV_IN + LRU_WIDTH], p[..., RWKV_IN + LRU_WIDTH:],
                            e_conv_w[i], e_conv_b[i], e_lru_wa[i], e_lru_ba[i], e_lru_wx[i], e_lru_bx[i],
                            e_lru_lambda[i])
            h = h + jnp.concatenate([y_a, y_b], axis=-1) @ e_w_out[i]
        else:
            p = rms_norm(h, o_norm[i]) @ o_w_in[i]
            h = h + gla_mix(p, o_g_up[i], o_g_b[i], o_head_gain[i]) @ o_w_out[i]
        h = h + hier_moe(rms_norm(h, m_norm[layer]), m_w_group[layer], m_b_group[layer], m_w_router[layer],
                         m_b_router[layer], m_w_gate[layer], m_w_up[layer], m_w_down[layer])
    return rms_norm(h[:, N_META:], final_norm)
```

```python
import functools

import jax
import jax.numpy as jnp
from jax import lax
from jax.experimental import pallas as pl
from jax.experimental.pallas import tpu as pltpu

F32 = jnp.float32
BF16 = jnp.bfloat16

N_META = 16
NORM_EPS = 1e-6
RWKV_HEADS = 8
HEAD_DIM = 64
RWKV_WIDTH = 512
LORA_W = 64
LORA_A = 64
LORA_G = 128
RWKV_IN = 3 * RWKV_WIDTH + LORA_W + LORA_A + LORA_G
GN_EPS = 64e-5
LRU_WIDTH = 512
LRU_C = 8.0
GLA_HEADS = 4
GLA_DK = 128
GLA_DV = 256
GLA_KEY_WIDTH = GLA_HEADS * GLA_DK
GLA_VAL_WIDTH = GLA_HEADS * GLA_DV
GLA_LORA = 16
GLA_TAU = 16.0
GLA_SUB = 16
N_GROUPS = 4
EXPERTS_PER_GROUP = 8
N_EXPERTS = N_GROUPS * EXPERTS_PER_GROUP
EXPERT_FF = 256

LANES = 128
SUBLANES = 8
VMEM_LIMIT = 48 * 1024 * 1024


def _pick_tile(n, target, mult=16):
    best = None
    for t in range(mult, min(n, target) + 1, mult):
        if n % t == 0:
            best = t
    assert best is not None, (n, target, mult)
    return best


def _params(*sem):
    return pltpu.CompilerParams(dimension_semantics=sem, vmem_limit_bytes=VMEM_LIMIT)


def _softplus(x):
    return jnp.maximum(x, 0.0) + jnp.log(1.0 + jnp.exp(-jnp.abs(x)))


def _sigmoid(x):
    return 1.0 / (1.0 + jnp.exp(-x))


def _block_ones(width, block, dtype):
    r = lax.broadcasted_iota(jnp.int32, (width, width), 0) // block
    c = lax.broadcasted_iota(jnp.int32, (width, width), 1) // block
    return (r == c).astype(dtype)


def _const_spec(shape):
    zeros = (0,) * len(shape)
    return pl.BlockSpec(shape, lambda *_: zeros)


def _norm_proj_kernel(x_ref, g_ref, w_ref, *o_refs):
    x = x_ref[...]
    y = x * lax.rsqrt(jnp.mean(x * x, axis=-1, keepdims=True) + NORM_EPS) * g_ref[...]
    y = y.astype(BF16)
    off = 0
    for o_ref in o_refs:
        width = o_ref.shape[-1]
        o_ref[...] = jnp.dot(y, w_ref[:, off:off + width], preferred_element_type=F32).astype(o_ref.dtype)
        off += width


def _norm_proj(x, gain, w, splits, tm):
    n, d = x.shape
    f = w.shape[1]
    assert sum(splits) == f and all(s % LANES == 0 for s in splits)
    return pl.pallas_call(
        _norm_proj_kernel,
        grid=(n // tm,),
        in_specs=[pl.BlockSpec((tm, d), lambda i: (i, 0)), _const_spec((1, d)), _const_spec((d, f))],
        out_specs=[pl.BlockSpec((tm, s), lambda i: (i, 0)) for s in splits],
        out_shape=[jax.ShapeDtypeStruct((n, s), F32) for s in splits],
        compiler_params=_params("parallel"),
        name="norm_proj",
    )(x, gain.reshape(1, d), w.astype(BF16))


def _out_proj_kernel(*refs):
    h_ref, o_ref = refs[0], refs[-1]
    pairs = refs[1:-1]
    acc = h_ref[...]
    for y_ref, w_ref in zip(pairs[0::2], pairs[1::2]):
        acc = acc + jnp.dot(y_ref[...], w_ref[...], preferred_element_type=F32)
    o_ref[...] = acc


def _out_proj(h, ys, ws, tm):
    n, d = h.shape
    in_specs = [pl.BlockSpec((tm, d), lambda i: (i, 0))]
    args = [h]
    for y, w in zip(ys, ws):
        in_specs += [pl.BlockSpec((tm, y.shape[1]), lambda i: (i, 0)), _const_spec(w.shape)]
        args += [y, w.astype(BF16)]
    return pl.pallas_call(
        _out_proj_kernel,
        grid=(n // tm,),
        in_specs=in_specs,
        out_specs=pl.BlockSpec((tm, d), lambda i: (i, 0)),
        out_shape=jax.ShapeDtypeStruct((n, d), F32),
        compiler_params=_params("parallel"),
        name="out_proj",
    )(*args)


def _rwkv_prep_kernel(p_ref, mu_ref, lora_ref, dec0_ref, iclr0_ref, gup_ref, kk_ref, ka_ref, rk_ref, ones_ref,
                      r_o, w_o, k_o, v_o, kkn_o, b_o, g_o, bonus_o, carry_ref):
    tt = p_ref.shape[0]
    width = RWKV_WIDTH

    @pl.when(pl.program_id(1) == 0)
    def _():
        carry_ref[...] = jnp.zeros_like(carry_ref)

    p = p_ref[...]
    row = lax.broadcasted_iota(jnp.int32, (tt, 1), 0)
    prev = jnp.where(row == 0, carry_ref[...], pltpu.roll(p, 1, axis=0))
    carry_ref[...] = p[tt - 1:tt, :]
    xs = p + mu_ref[...] * (prev - p)
    r = xs[:, 0:width]
    k = xs[:, width:2 * width]
    v = xs[:, 2 * width:3 * width]
    wa = xs[:, 3 * width:3 * width + LANES]
    gd = xs[:, 3 * width + LANES:]
    lane = lax.broadcasted_iota(jnp.int32, (1, LANES), 1)
    wa = jnp.where(lane < LORA_W, jnp.tanh(wa), wa)
    lora = jnp.dot(wa.astype(BF16), lora_ref[...], preferred_element_type=F32)
    wlog = -_softplus(-(dec0_ref[...] + lora[:, :width])) - 0.5
    decay = jnp.exp(-jnp.exp(wlog))
    a = _sigmoid(iclr0_ref[...] + lora[:, width:])
    g = jnp.dot(_sigmoid(gd).astype(BF16), gup_ref[...], preferred_element_type=F32)
    ones = ones_ref[...]
    kraw = k * kk_ref[...]
    ss = jnp.dot((kraw * kraw).astype(BF16), ones, preferred_element_type=F32)
    kkn = kraw / jnp.maximum(jnp.sqrt(ss), 1e-12)
    k_mod = k * (1.0 + (a - 1.0) * ka_ref[...])
    bonus = jnp.dot((r * k_mod * rk_ref[...]).astype(BF16), ones, preferred_element_type=F32) * v
    r_o[...] = r
    w_o[...] = decay
    k_o[...] = k_mod
    v_o[...] = v
    kkn_o[...] = kkn
    b_o[...] = kkn * a
    g_o[...] = g
    bonus_o[...] = bonus


def _rwkv_prep(p, bsz, t, tt, mu, decay0, decay_up, iclr0, iclr_up, gate_up, k_k, k_a, r_k):
    nt = t // tt
    width = RWKV_WIDTH
    lora_w = jnp.zeros((LANES, 2 * width), F32)
    lora_w = lora_w.at[:LORA_W, :width].set(decay_up).at[LORA_W:, width:].set(iclr_up)
    row = lambda x: x.reshape(1, -1)
    tm_spec = pl.BlockSpec((tt, width), lambda b, i: (i, b))
    outs = pl.pallas_call(
        _rwkv_prep_kernel,
        grid=(bsz, nt),
        in_specs=[pl.BlockSpec((tt, RWKV_IN), lambda b, i: (b * nt + i, 0)),
                  _const_spec((1, RWKV_IN)), _const_spec((LANES, 2 * width)), _const_spec((1, width)),
                  _const_spec((1, width)), _const_spec((LORA_G, width)), _const_spec((1, width)),
                  _const_spec((1, width)), _const_spec((1, width)), _const_spec((width, width))],
        out_specs=[tm_spec] * 8,
        out_shape=[jax.ShapeDtypeStruct((t, bsz * width), F32)] * 8,
        scratch_shapes=[pltpu.VMEM((1, RWKV_IN), F32)],
        compiler_params=_params("parallel", "arbitrary"),
        name="rwkv_prep",
    )(p, row(mu), lora_w.astype(BF16), row(decay0), row(iclr0), gate_up.astype(BF16), row(k_k), row(k_a),
      row(r_k), _block_ones(width, HEAD_DIM, BF16))
    return outs


def _rwkv_scan_kernel(r_ref, w_ref, k_ref, v_ref, kk_ref, b_ref, ones_ref, eye_ref, o_ref, s_ref, *, nb):
    ts = r_ref.shape[0]
    width = RWKV_WIDTH
    half = width // 2

    @pl.when(pl.program_id(0) == 0)
    def _():
        s_ref[...] = jnp.zeros_like(s_ref)

    ones = ones_ref[...]
    eye = eye_ref[...]

    def seg_sum(x, two_pass):
        hi = x.astype(BF16)
        parts = [hi]
        if two_pass:
            parts.append((x - hi.astype(F32)).astype(BF16))
        cols = []
        for c in range(2):
            acc = None
            for part in parts:
                d = jnp.dot(part[:, c * half:(c + 1) * half], ones, preferred_element_type=F32)
                acc = d if acc is None else acc + d
            cols.append(acc)
        return jnp.concatenate(cols, axis=1)

    def rows(ref, t):
        return jnp.concatenate(
            [jnp.broadcast_to(ref[pl.ds(t, 1), b * width:(b + 1) * width], (HEAD_DIM, width)) for b in range(nb)],
            axis=0)

    eye_all = jnp.concatenate([eye] * nb, axis=0)

    def step(t, carry):
        s = s_ref[...]
        sa = seg_sum(s * rows(kk_ref, t), True)
        vcol = seg_sum(eye_all * rows(v_ref, t), False)
        s = s * rows(w_ref, t) - sa * rows(b_ref, t) + vcol * rows(k_ref, t)
        s_ref[...] = s
        o = seg_sum(s * rows(r_ref, t), False) * eye_all
        for b in range(nb):
            o_ref[pl.ds(t, 1), b * width:(b + 1) * width] = jnp.sum(
                o[b * HEAD_DIM:(b + 1) * HEAD_DIM], axis=0, keepdims=True)
        return carry

    lax.fori_loop(0, ts, step, 0)


def _rwkv_scan(r, w, k, v, kk, b, bsz, t, ts):
    width = RWKV_WIDTH
    eye = jnp.tile(jnp.eye(HEAD_DIM, dtype=F32), (1, RWKV_HEADS))
    spec = pl.BlockSpec((ts, bsz * width), lambda i: (i, 0))
    return pl.pallas_call(
        functools.partial(_rwkv_scan_kernel, nb=bsz),
        grid=(t // ts,),
        in_specs=[spec] * 6 + [_const_spec((width // 2, width // 2)), _const_spec((HEAD_DIM, width))],
        out_specs=spec,
        out_shape=jax.ShapeDtypeStruct((t, bsz * width), F32),
        scratch_shapes=[pltpu.VMEM((bsz * HEAD_DIM, width), F32)],
        compiler_params=_params("arbitrary"),
        name="rwkv_scan",
    )(r, w, k, v, kk, b, _block_ones(width // 2, HEAD_DIM, BF16), eye)


def _rwkv_post_kernel(o_ref, bonus_ref, g_ref, lng_ref, lnb_ref, ones_ref, y_ref):
    ones = ones_ref[...]

    def seg_mean(x):
        hi = x.astype(BF16)
        lo = (x - hi.astype(F32)).astype(BF16)
        return (jnp.dot(hi, ones, preferred_element_type=F32)
                + jnp.dot(lo, ones, preferred_element_type=F32)) * (1.0 / HEAD_DIM)

    o = o_ref[...]
    d = o - seg_mean(o)
    var = seg_mean(d * d)
    y = d * lax.rsqrt(var + GN_EPS) * lng_ref[...] + lnb_ref[...]
    y_ref[...] = ((y + bonus_ref[...]) * g_ref[...]).astype(y_ref.dtype)


def _rwkv_post(o, bonus, g, ln_g, ln_b, bsz, t, tt):
    nt = t // tt
    width = RWKV_WIDTH
    tm_spec = pl.BlockSpec((tt, width), lambda b, i: (i, b))
    return pl.pallas_call(
        _rwkv_post_kernel,
        grid=(bsz, nt),
        in_specs=[tm_spec] * 3 + [_const_spec((1, width))] * 2 + [_const_spec((width, width))],
        out_specs=pl.BlockSpec((tt, width), lambda b, i: (b * nt + i, 0)),
        out_shape=jax.ShapeDtypeStruct((bsz * t, width), BF16),
        compiler_params=_params("parallel", "parallel"),
        name="rwkv_post",
    )(o, bonus, g, ln_g.reshape(1, -1), ln_b.reshape(1, -1), _block_ones(width, HEAD_DIM, BF16))


def _lru_kernel(xb_ref, gb_ref, cw_ref, cb_ref, wa_ref, ba_ref, wx_ref, bx_ref, lam_ref, y_ref,
                hist_ref, hlast_ref, a_s, u_s):
    tt = xb_ref.shape[0]
    sub = SUBLANES

    @pl.when(pl.program_id(1) == 0)
    def _():
        hist_ref[...] = jnp.zeros_like(hist_ref)
        hlast_ref[...] = jnp.zeros_like(hlast_ref)

    xb = xb_ref[...]
    hist = hist_ref[...]
    hist_ref[...] = xb[tt - sub:, :]
    row8 = lax.broadcasted_iota(jnp.int32, (sub, 1), 0)
    cw = cw_ref[...]
    xc = xb * cw[3:4, :] + cb_ref[...]
    for s in (1, 2, 3):
        rolled = pltpu.roll(xb, s, axis=0)
        top = jnp.where(row8 < s, pltpu.roll(hist, s, axis=0), rolled[:sub, :])
        shifted = jnp.concatenate([top, rolled[sub:, :]], axis=0)
        xc = xc + shifted * cw[3 - s:4 - s, :]
    xcb = xc.astype(BF16)
    gate_r = _sigmoid(jnp.dot(xcb, wa_ref[...], preferred_element_type=F32) + ba_ref[...])
    gate_i = _sigmoid(jnp.dot(xcb, wx_ref[...], preferred_element_type=F32) + bx_ref[...])
    log_a = -LRU_C * gate_r * _softplus(-lam_ref[...])
    a_s[...] = jnp.exp(log_a)
    u_s[...] = jnp.sqrt(1.0 - jnp.exp(2.0 * log_a)) * gate_i * xc

    def group(j, h_prev):
        r0 = pl.multiple_of(j * sub, sub)
        a = a_s[pl.ds(r0, sub), :]
        u = u_s[pl.ds(r0, sub), :]
        for d in (1, 2, 4):
            keep = row8 >= d
            u = jnp.where(keep, a * pltpu.roll(u, d, axis=0) + u, u)
            a = jnp.where(keep, a * pltpu.roll(a, d, axis=0), a)
        h = u + a * h_prev
        u_s[pl.ds(r0, sub), :] = h
        return jnp.broadcast_to(h[sub - 1:sub, :], h.shape)

    h_last = lax.fori_loop(0, tt // sub, group, hlast_ref[...])
    hlast_ref[...] = h_last
    y_ref[...] = (u_s[...] * jax.nn.gelu(gb_ref[...])).astype(y_ref.dtype)


def _block_diag(w):
    nb, bi, bo = w.shape
    out = jnp.zeros((nb * bi, nb * bo), w.dtype)
    for i in range(nb):
        out = out.at[i * bi:(i + 1) * bi, i * bo:(i + 1) * bo].set(w[i])
    return out


def _lru(xb, gb, bsz, t, tt, conv_w, conv_b, wa, ba, wx, bx, lam):
    nt = t // tt
    width = LRU_WIDTH
    row = lambda x: x.reshape(1, -1)
    spec = pl.BlockSpec((tt, width), lambda b, i: (b * nt + i, 0))
    return pl.pallas_call(
        _lru_kernel,
        grid=(bsz, nt),
        in_specs=[spec, spec, _const_spec((4, width)), _const_spec((1, width)), _const_spec((width, width)),
                  _const_spec((1, width)), _const_spec((width, width)), _const_spec((1, width)),
                  _const_spec((1, width))],
        out_specs=spec,
        out_shape=jax.ShapeDtypeStruct((bsz * t, width), BF16),
        scratch_shapes=[pltpu.VMEM((SUBLANES, width), F32), pltpu.VMEM((SUBLANES, width), F32),
                        pltpu.VMEM((tt, width), F32), pltpu.VMEM((tt, width), F32)],
        compiler_params=_params("parallel", "arbitrary"),
        name="rglru",
    )(xb, gb, conv_w, row(conv_b), _block_diag(wa).astype(BF16), row(ba), _block_diag(wx).astype(BF16), row(bx),
      row(lam))


def _gla_kernel(q_ref, k_ref, v_ref, r_ref, gd_ref, gup_ref, gb_ref, gain_ref, tri_ref, y_ref, st_ref, cum_s):
    tt = q_ref.shape[0]
    sub = GLA_SUB

    @pl.when(pl.program_id(1) == 0)
    def _():
        st_ref[...] = jnp.zeros_like(st_ref)

    logits = jnp.dot(gd_ref[...].astype(BF16), gup_ref[...], preferred_element_type=F32) + gb_ref[...]
    log_a = -_softplus(-logits) * (1.0 / GLA_TAU)
    cum_s[...] = jnp.dot(tri_ref[...], log_a, preferred_element_type=F32, precision=lax.Precision.HIGHEST)
    t_i = lax.broadcasted_iota(jnp.int32, (sub, sub, 1), 0)
    s_i = lax.broadcasted_iota(jnp.int32, (sub, sub, 1), 1)
    causal = s_i <= t_i
    scale = GLA_DK ** -0.5
    contract_last = (((1,), (1,)), ((), ()))
    contract_first = (((0,), (0,)), ((), ()))

    def chunk(c, carry):
        r0 = pl.multiple_of(c * sub, sub)
        q = q_ref[pl.ds(r0, sub), :] * scale
        k = k_ref[pl.ds(r0, sub), :]
        v = v_ref[pl.ds(r0, sub), :]
        cum = cum_s[pl.ds(r0, sub), :]
        gate = gain_ref[...] * (lambda x: x * _sigmoid(x))(r_ref[pl.ds(r0, sub), :])
        for h in range(GLA_HEADS):
            ks = slice(h * GLA_DK, (h + 1) * GLA_DK)
            vs = slice(h * GLA_DV, (h + 1) * GLA_DV)
            qh, kh, ch, vh = q[:, ks], k[:, ks], cum[:, ks], v[:, vs]
            diff = ch[:, None, :] - ch[None, :, :]
            dec = jnp.where(causal, jnp.exp(jnp.minimum(diff, 0.0)), 0.0)
            sc = jnp.sum(qh[:, None, :] * kh[None, :, :] * dec, axis=-1, keepdims=True)
            intra = jnp.sum(sc * vh[None, :, :], axis=1)
            st = st_ref[h]
            inter = lax.dot_general((qh * jnp.exp(ch)).astype(BF16), st.astype(BF16), contract_last,
                                    preferred_element_type=F32)
            o = intra + inter
            last = ch[sub - 1:sub, :]
            khat = kh * jnp.exp(last - ch)
            st_ref[h] = st * jnp.exp(last) + lax.dot_general(vh.astype(BF16), khat.astype(BF16), contract_first,
                                                             preferred_element_type=F32)
            o = o * lax.rsqrt(jnp.mean(o * o, axis=-1, keepdims=True) + NORM_EPS)
            y_ref[pl.ds(r0, sub), vs] = (o * gate[:, vs]).astype(y_ref.dtype)
        return carry

    lax.fori_loop(0, tt // sub, chunk, 0)


def _gla(q, k, v, r, gd, bsz, t, tt, g_up, g_b, head_gain):
    nt = t // tt
    kw, vw = GLA_KEY_WIDTH, GLA_VAL_WIDTH
    gup = jnp.zeros((LANES, kw), F32).at[:GLA_LORA].set(g_up)
    i = jnp.arange(tt)
    tri = ((i[None, :] <= i[:, None]) & (i[None, :] // GLA_SUB == i[:, None] // GLA_SUB)).astype(F32)
    spec = lambda w: pl.BlockSpec((tt, w), lambda b, j: (b * nt + j, 0))
    return pl.pallas_call(
        _gla_kernel,
        grid=(bsz, nt),
        in_specs=[spec(kw), spec(kw), spec(vw), spec(vw), spec(LANES), _const_spec((LANES, kw)),
                  _const_spec((1, kw)), _const_spec((1, vw)), _const_spec((tt, tt))],
        out_specs=spec(vw),
        out_shape=jax.ShapeDtypeStruct((bsz * t, vw), BF16),
        scratch_shapes=[pltpu.VMEM((GLA_HEADS, GLA_DV, GLA_DK), F32), pltpu.VMEM((tt, kw), F32)],
        compiler_params=_params("parallel", "arbitrary"),
        name="gla",
    )(q, k, v, r, gd, gup.astype(BF16), g_b.reshape(1, -1), head_gain.reshape(1, -1), tri)


def _moe_kernel(h_ref, gain_ref, wr_ref, br_ref, wg_ref, wu_ref, wd_ref, o_ref, xn_s, comb_s):
    e = pl.program_id(1)
    lane = lax.broadcasted_iota(jnp.int32, (1, LANES), 1)

    @pl.when(e == 0)
    def _():
        x = h_ref[...]
        xn = x * lax.rsqrt(jnp.mean(x * x, axis=-1, keepdims=True) + NORM_EPS) * gain_ref[...]
        xn_s[...] = xn.astype(BF16)
        logits = jnp.dot(xn, wr_ref[...], preferred_element_type=F32, precision=lax.Precision.HIGHEST) + br_ref[...]
        neg = -jnp.inf
        far = 4 * LANES
        first = lambda hit: jnp.min(jnp.where(hit, lane, far), axis=-1, keepdims=True)
        is_group = lane < N_GROUPS
        gl = jnp.where(is_group, logits, neg)
        gmax = jnp.max(gl, axis=-1, keepdims=True)
        g_idx = first(gl == gmax)
        g_prob = 1.0 / jnp.sum(jnp.where(is_group, jnp.exp(logits - gmax), 0.0), axis=-1, keepdims=True)
        in_group = (lane >= N_GROUPS) & (lane < N_GROUPS + N_EXPERTS) & (((lane - N_GROUPS) >> 3) == g_idx)
        el = jnp.where(in_group, logits, neg)
        v1 = jnp.max(el, axis=-1, keepdims=True)
        i1 = first(el == v1)
        el2 = jnp.where(lane == i1, neg, el)
        v2 = jnp.max(el2, axis=-1, keepdims=True)
        i2 = first(el2 == v2)
        e21 = jnp.exp(v2 - v1)
        w1 = g_prob / (1.0 + e21)
        comb_s[...] = jnp.where(lane == i1, w1, 0.0) + jnp.where(lane == i2, w1 * e21, 0.0)
        o_ref[...] = x

    xn = xn_s[...]
    gate = jnp.dot(xn, wg_ref[0], preferred_element_type=F32)
    up = jnp.dot(xn, wu_ref[0], preferred_element_type=F32)
    ce = jnp.sum(jnp.where(lane == e + N_GROUPS, comb_s[...], 0.0), axis=-1, keepdims=True)
    hid = gate * _sigmoid(gate) * up * ce
    o_ref[...] += jnp.dot(hid.astype(BF16), wd_ref[0], preferred_element_type=F32)


def _moe(h, gain, w_group, b_group, w_router, b_router, w_gate, w_up, w_down, tm):
    n, d = h.shape
    wr = jnp.zeros((d, LANES), F32).at[:, :N_GROUPS].set(w_group).at[:, N_GROUPS:N_GROUPS + N_EXPERTS].set(w_router)
    br = jnp.zeros((1, LANES), F32).at[0, :N_GROUPS].set(b_group).at[0, N_GROUPS:N_GROUPS + N_EXPERTS].set(b_router)
    return pl.pallas_call(
        _moe_kernel,
        grid=(n // tm, N_EXPERTS),
        in_specs=[pl.BlockSpec((tm, d), lambda i, e: (i, 0)), _const_spec((1, d)), _const_spec((d, LANES)),
                  _const_spec((1, LANES)),
                  pl.BlockSpec((1, d, EXPERT_FF), lambda i, e: (e, 0, 0)),
                  pl.BlockSpec((1, d, EXPERT_FF), lambda i, e: (e, 0, 0)),
                  pl.BlockSpec((1, EXPERT_FF, d), lambda i, e: (e, 0, 0))],
        out_specs=pl.BlockSpec((tm, d), lambda i, e: (i, 0)),
        out_shape=jax.ShapeDtypeStruct((n, d), F32),
        scratch_shapes=[pltpu.VMEM((tm, d), BF16), pltpu.VMEM((tm, LANES), F32)],
        compiler_params=_params("parallel", "arbitrary"),
        name="moe",
    )(h, gain.reshape(1, d), wr, br, w_gate.astype(BF16), w_up.astype(BF16), w_down.astype(BF16))


def _final_norm_kernel(x_ref, g_ref, o_ref):
    x = x_ref[...]
    o_ref[...] = x * lax.rsqrt(jnp.mean(x * x, axis=-1, keepdims=True) + NORM_EPS) * g_ref[...]


def _final_norm(h, gain, tm):
    n, d = h.shape
    return pl.pallas_call(
        _final_norm_kernel,
        grid=(n // tm,),
        in_specs=[pl.BlockSpec((tm, d), lambda i: (i, 0)), _const_spec((1, d))],
        out_specs=pl.BlockSpec((tm, d), lambda i: (i, 0)),
        out_shape=jax.ShapeDtypeStruct((n, d), F32),
        compiler_params=_params("parallel"),
        name="final_norm",
    )(h, gain.reshape(1, d))


def _even_mixer(h, bsz, t, tm, tt, ts, norm, w_in, mu, decay0, decay_up, iclr0, iclr_up, gate_up, k_k, k_a, r_k,
                ln_g, ln_b, conv_w, conv_b, lru_wa, lru_ba, lru_wx, lru_bx, lru_lambda, w_out):
    p, xb, gb = _norm_proj(h, norm, w_in, (RWKV_IN, LRU_WIDTH, LRU_WIDTH), tm)
    r, w, k, v, kk, b, g, bonus = _rwkv_prep(p, bsz, t, tt, mu, decay0, decay_up, iclr0, iclr_up, gate_up, k_k,
                                             k_a, r_k)
    o = _rwkv_scan(r, w, k, v, kk, b, bsz, t, ts)
    y_a = _rwkv_post(o, bonus, g, ln_g, ln_b, bsz, t, tt)
    y_b = _lru(xb, gb, bsz, t, tt, conv_w, conv_b, lru_wa, lru_ba, lru_wx, lru_bx, lru_lambda)
    return _out_proj(h, (y_a, y_b), (w_out[:RWKV_WIDTH], w_out[RWKV_WIDTH:]), tm)


def _odd_mixer(h, bsz, t, tm, tt, norm, w_in, g_up, g_b, head_gain, w_out):
    kw, vw = GLA_KEY_WIDTH, GLA_VAL_WIDTH
    d = w_in.shape[0]
    w_packed = jnp.concatenate(
        [w_in[:, :2 * kw + vw], w_in[:, 2 * kw + vw + GLA_LORA:], w_in[:, 2 * kw + vw:2 * kw + vw + GLA_LORA],
         jnp.zeros((d, LANES - GLA_LORA), w_in.dtype)], axis=1)
    q, k, v, r, gd = _norm_proj(h, norm, w_packed, (kw, kw, vw, vw, LANES), tm)
    y = _gla(q, k, v, r, gd, bsz, t, tt, g_up, g_b, head_gain)
    return _out_proj(h, (y,), (w_out,), tm)


def kernel(x, meta, e_norm, e_w_in, e_mu, e_decay0, e_decay_up, e_iclr0, e_iclr_up, e_gate_up, e_k_k, e_k_a, e_r_k, e_ln_g, e_ln_b, e_conv_w, e_conv_b, e_lru_wa, e_lru_ba, e_lru_wx, e_lru_bx, e_lru_lambda, e_w_out, o_norm, o_w_in, o_g_up, o_g_b, o_head_gain, o_w_out, m_norm, m_w_group, m_b_group, m_w_router, m_b_router, m_w_gate, m_w_up, m_w_down, final_norm):
    bsz, seq, d = x.shape
    t = seq + N_META
    n = bsz * t
    tm = _pick_tile(n, 384)
    tt = _pick_tile(t, 432)
    ts = _pick_tile(t, 48)
    depth = m_norm.shape[0]
    h = jnp.concatenate([jnp.broadcast_to(meta[None].astype(x.dtype), (bsz, N_META, d)), x], axis=1).reshape(n, d)
    for layer in range(depth):
        i = layer // 2
        if layer % 2 == 0:
            h = _even_mixer(h, bsz, t, tm, tt, ts, e_norm[i], e_w_in[i], e_mu[i], e_decay0[i], e_decay_up[i],
                            e_iclr0[i], e_iclr_up[i], e_gate_up[i], e_k_k[i], e_k_a[i], e_r_k[i], e_ln_g[i],
                            e_ln_b[i], e_conv_w[i], e_conv_b[i], e_lru_wa[i], e_lru_ba[i], e_lru_wx[i],
                            e_lru_bx[i], e_lru_lambda[i], e_w_out[i])
        else:
            h = _odd_mixer(h, bsz, t, tm, tt, o_norm[i], o_w_in[i], o_g_up[i], o_g_b[i], o_head_gain[i], o_w_out[i])
        h = _moe(h, m_norm[layer], m_w_group[layer], m_b_group[layer], m_w_router[layer], m_b_router[layer],
                 m_w_gate[layer], m_w_up[layer], m_w_down[layer], tm)
    out = _final_norm(h, final_norm, tm)
    return out.reshape(bsz, t, d)[:, N_META:]
```

```python
import functools

import jax
import jax.numpy as jnp
from jax import lax
from jax.experimental import pallas as pl
from jax.experimental.pallas import tpu as pltpu

F32 = jnp.float32
BF16 = jnp.bfloat16

N_META = 16
NORM_EPS = 1e-6
RWKV_HEADS = 8
HEAD_DIM = 64
RWKV_WIDTH = 512
LORA_W = 64
LORA_A = 64
LORA_G = 128
RWKV_IN = 3 * RWKV_WIDTH + LORA_W + LORA_A + LORA_G
GN_EPS = 64e-5
LRU_WIDTH = 512
LRU_C = 8.0
GLA_HEADS = 4
GLA_DK = 128
GLA_DV = 256
GLA_KEY_WIDTH = GLA_HEADS * GLA_DK
GLA_VAL_WIDTH = GLA_HEADS * GLA_DV
GLA_LORA = 16
GLA_TAU = 16.0
GLA_SUB = 16
N_GROUPS = 4
EXPERTS_PER_GROUP = 8
N_EXPERTS = N_GROUPS * EXPERTS_PER_GROUP
EXPERT_FF = 256
MOE_TILE = 256

LANES = 128
SUBLANES = 8
VMEM_LIMIT = 48 * 1024 * 1024


def _pick_tile(n, target, mult=16):
    best = None
    for t in range(mult, min(n, target) + 1, mult):
        if n % t == 0:
            best = t
    assert best is not None, (n, target, mult)
    return best


def _params(*sem):
    return pltpu.CompilerParams(dimension_semantics=sem, vmem_limit_bytes=VMEM_LIMIT)


def _softplus(x):
    return jnp.maximum(x, 0.0) + jnp.log(1.0 + jnp.exp(-jnp.abs(x)))


def _sigmoid(x):
    return 1.0 / (1.0 + jnp.exp(-x))


def _block_ones(width, block, dtype):
    r = lax.broadcasted_iota(jnp.int32, (width, width), 0) // block
    c = lax.broadcasted_iota(jnp.int32, (width, width), 1) // block
    return (r == c).astype(dtype)


def _const_spec(shape):
    zeros = (0,) * len(shape)
    return pl.BlockSpec(shape, lambda *_: zeros)


def _norm_proj_kernel(x_ref, g_ref, w_ref, *o_refs):
    x = x_ref[...]
    y = x * lax.rsqrt(jnp.mean(x * x, axis=-1, keepdims=True) + NORM_EPS) * g_ref[...]
    y = y.astype(BF16)
    off = 0
    for o_ref in o_refs:
        width = o_ref.shape[-1]
        o_ref[...] = jnp.dot(y, w_ref[:, off:off + width], preferred_element_type=F32).astype(o_ref.dtype)
        off += width


def _norm_proj(x, gain, w, splits, tm):
    n, d = x.shape
    f = w.shape[1]
    assert sum(splits) == f and all(s % LANES == 0 for s in splits)
    return pl.pallas_call(
        _norm_proj_kernel,
        grid=(n // tm,),
        in_specs=[pl.BlockSpec((tm, d), lambda i: (i, 0)), _const_spec((1, d)), _const_spec((d, f))],
        out_specs=[pl.BlockSpec((tm, s), lambda i: (i, 0)) for s in splits],
        out_shape=[jax.ShapeDtypeStruct((n, s), F32) for s in splits],
        compiler_params=_params("parallel"),
        name="norm_proj",
    )(x, gain.reshape(1, d), w.astype(BF16))


def _out_proj_kernel(*refs):
    h_ref, o_ref = refs[0], refs[-1]
    pairs = refs[1:-1]
    acc = h_ref[...]
    for y_ref, w_ref in zip(pairs[0::2], pairs[1::2]):
        acc = acc + jnp.dot(y_ref[...], w_ref[...], preferred_element_type=F32)
    o_ref[...] = acc


def _out_proj(h, ys, ws, tm):
    n, d = h.shape
    in_specs = [pl.BlockSpec((tm, d), lambda i: (i, 0))]
    args = [h]
    for y, w in zip(ys, ws):
        in_specs += [pl.BlockSpec((tm, y.shape[1]), lambda i: (i, 0)), _const_spec(w.shape)]
        args += [y, w.astype(BF16)]
    return pl.pallas_call(
        _out_proj_kernel,
        grid=(n // tm,),
        in_specs=in_specs,
        out_specs=pl.BlockSpec((tm, d), lambda i: (i, 0)),
        out_shape=jax.ShapeDtypeStruct((n, d), F32),
        compiler_params=_params("parallel"),
        name="out_proj",
    )(*args)


def _rwkv_prep_kernel(p_ref, mu_ref, lora_ref, dec0_ref, iclr0_ref, gup_ref, kk_ref, ka_ref, rk_ref, ones_ref,
                      r_o, w_o, k_o, v_o, kkn_o, b_o, g_o, bonus_o, carry_ref):
    tt = p_ref.shape[0]
    width = RWKV_WIDTH

    @pl.when(pl.program_id(1) == 0)
    def _():
        carry_ref[...] = jnp.zeros_like(carry_ref)

    p = p_ref[...]
    row = lax.broadcasted_iota(jnp.int32, (tt, 1), 0)
    prev = jnp.where(row == 0, carry_ref[...], pltpu.roll(p, 1, axis=0))
    carry_ref[...] = p[tt - 1:tt, :]
    xs = p + mu_ref[...] * (prev - p)
    r = xs[:, 0:width]
    k = xs[:, width:2 * width]
    v = xs[:, 2 * width:3 * width]
    wa = xs[:, 3 * width:3 * width + LANES]
    gd = xs[:, 3 * width + LANES:]
    lane = lax.broadcasted_iota(jnp.int32, (1, LANES), 1)
    wa = jnp.where(lane < LORA_W, jnp.tanh(wa), wa)
    lora = jnp.dot(wa.astype(BF16), lora_ref[...], preferred_element_type=F32)
    wlog = -_softplus(-(dec0_ref[...] + lora[:, :width])) - 0.5
    decay = jnp.exp(-jnp.exp(wlog))
    a = _sigmoid(iclr0_ref[...] + lora[:, width:])
    g = jnp.dot(_sigmoid(gd).astype(BF16), gup_ref[...], preferred_element_type=F32)
    ones = ones_ref[...]
    kraw = k * kk_ref[...]
    ss = jnp.dot((kraw * kraw).astype(BF16), ones, preferred_element_type=F32)
    kkn = kraw / jnp.maximum(jnp.sqrt(ss), 1e-12)
    k_mod = k * (1.0 + (a - 1.0) * ka_ref[...])
    bonus = jnp.dot((r * k_mod * rk_ref[...]).astype(BF16), ones, preferred_element_type=F32) * v
    r_o[...] = r
    w_o[...] = decay
    k_o[...] = k_mod
    v_o[...] = v
    kkn_o[...] = kkn
    b_o[...] = kkn * a
    g_o[...] = g
    bonus_o[...] = bonus


def _rwkv_prep(p, bsz, t, tt, mu, decay0, decay_up, iclr0, iclr_up, gate_up, k_k, k_a, r_k):
    nt = t // tt
    width = RWKV_WIDTH
    lora_w = jnp.zeros((LANES, 2 * width), F32)
    lora_w = lora_w.at[:LORA_W, :width].set(decay_up).at[LORA_W:, width:].set(iclr_up)
    row = lambda x: x.reshape(1, -1)
    tm_spec = pl.BlockSpec((tt, width), lambda b, i: (i, b))
    outs = pl.pallas_call(
        _rwkv_prep_kernel,
        grid=(bsz, nt),
        in_specs=[pl.BlockSpec((tt, RWKV_IN), lambda b, i: (b * nt + i, 0)),
                  _const_spec((1, RWKV_IN)), _const_spec((LANES, 2 * width)), _const_spec((1, width)),
                  _const_spec((1, width)), _const_spec((LORA_G, width)), _const_spec((1, width)),
                  _const_spec((1, width)), _const_spec((1, width)), _const_spec((width, width))],
        out_specs=[tm_spec] * 8,
        out_shape=[jax.ShapeDtypeStruct((t, bsz * width), F32)] * 8,
        scratch_shapes=[pltpu.VMEM((1, RWKV_IN), F32)],
        compiler_params=_params("parallel", "arbitrary"),
        name="rwkv_prep",
    )(p, row(mu), lora_w.astype(BF16), row(decay0), row(iclr0), gate_up.astype(BF16), row(k_k), row(k_a),
      row(r_k), _block_ones(width, HEAD_DIM, BF16))
    return outs


def _rwkv_scan_kernel(r_ref, w_ref, k_ref, v_ref, kk_ref, b_ref, ones_ref, eye_ref, o_ref, s_ref, *, nb):
    ts = r_ref.shape[0]
    width = RWKV_WIDTH
    half = width // 2

    @pl.when(pl.program_id(0) == 0)
    def _():
        s_ref[...] = jnp.zeros_like(s_ref)

    ones = ones_ref[...]
    eye = eye_ref[...]

    def seg_sum(x, two_pass):
        hi = x.astype(BF16)
        parts = [hi]
        if two_pass:
            parts.append((x - hi.astype(F32)).astype(BF16))
        cols = []
        for c in range(2):
            acc = None
            for part in parts:
                d = jnp.dot(part[:, c * half:(c + 1) * half], ones, preferred_element_type=F32)
                acc = d if acc is None else acc + d
            cols.append(acc)
        return jnp.concatenate(cols, axis=1)

    def rows(ref, t):
        return jnp.concatenate(
            [jnp.broadcast_to(ref[pl.ds(t, 1), b * width:(b + 1) * width], (HEAD_DIM, width)) for b in range(nb)],
            axis=0)

    eye_all = jnp.concatenate([eye] * nb, axis=0)

    def step(t, carry):
        s = s_ref[...]
        sa = seg_sum(s * rows(kk_ref, t), True)
        vcol = seg_sum(eye_all * rows(v_ref, t), False)
        s = s * rows(w_ref, t) - sa * rows(b_ref, t) + vcol * rows(k_ref, t)
        s_ref[...] = s
        o = seg_sum(s * rows(r_ref, t), False) * eye_all
        for b in range(nb):
            o_ref[pl.ds(t, 1), b * width:(b + 1) * width] = jnp.sum(
                o[b * HEAD_DIM:(b + 1) * HEAD_DIM], axis=0, keepdims=True)
        return carry

    lax.fori_loop(0, ts, step, 0)


def _rwkv_scan(r, w, k, v, kk, b, bsz, t, ts):
    width = RWKV_WIDTH
    eye = jnp.tile(jnp.eye(HEAD_DIM, dtype=F32), (1, RWKV_HEADS))
    spec = pl.BlockSpec((ts, bsz * width), lambda i: (i, 0))
    return pl.pallas_call(
        functools.partial(_rwkv_scan_kernel, nb=bsz),
        grid=(t // ts,),
        in_specs=[spec] * 6 + [_const_spec((width // 2, width // 2)), _const_spec((HEAD_DIM, width))],
        out_specs=spec,
        out_shape=jax.ShapeDtypeStruct((t, bsz * width), F32),
        scratch_shapes=[pltpu.VMEM((bsz * HEAD_DIM, width), F32)],
        compiler_params=_params("arbitrary"),
        name="rwkv_scan",
    )(r, w, k, v, kk, b, _block_ones(width // 2, HEAD_DIM, BF16), eye)


def _rwkv_post_kernel(o_ref, bonus_ref, g_ref, lng_ref, lnb_ref, ones_ref, y_ref):
    ones = ones_ref[...]

    def seg_mean(x):
        hi = x.astype(BF16)
        lo = (x - hi.astype(F32)).astype(BF16)
        return (jnp.dot(hi, ones, preferred_element_type=F32)
                + jnp.dot(lo, ones, preferred_element_type=F32)) * (1.0 / HEAD_DIM)

    o = o_ref[...]
    d = o - seg_mean(o)
    var = seg_mean(d * d)
    y = d * lax.rsqrt(var + GN_EPS) * lng_ref[...] + lnb_ref[...]
    y_ref[...] = ((y + bonus_ref[...]) * g_ref[...]).astype(y_ref.dtype)


def _rwkv_post(o, bonus, g, ln_g, ln_b, bsz, t, tt):
    nt = t // tt
    width = RWKV_WIDTH
    tm_spec = pl.BlockSpec((tt, width), lambda b, i: (i, b))
    return pl.pallas_call(
        _rwkv_post_kernel,
        grid=(bsz, nt),
        in_specs=[tm_spec] * 3 + [_const_spec((1, width))] * 2 + [_const_spec((width, width))],
        out_specs=pl.BlockSpec((tt, width), lambda b, i: (b * nt + i, 0)),
        out_shape=jax.ShapeDtypeStruct((bsz * t, width), BF16),
        compiler_params=_params("parallel", "parallel"),
        name="rwkv_post",
    )(o, bonus, g, ln_g.reshape(1, -1), ln_b.reshape(1, -1), _block_ones(width, HEAD_DIM, BF16))


def _lru_kernel(xb_ref, gb_ref, cw_ref, cb_ref, wa_ref, ba_ref, wx_ref, bx_ref, lam_ref, y_ref,
                hist_ref, hlast_ref, a_s, u_s):
    tt = xb_ref.shape[0]
    sub = SUBLANES

    @pl.when(pl.program_id(1) == 0)
    def _():
        hist_ref[...] = jnp.zeros_like(hist_ref)
        hlast_ref[...] = jnp.zeros_like(hlast_ref)

    xb = xb_ref[...]
    hist = hist_ref[...]
    hist_ref[...] = xb[tt - sub:, :]
    row8 = lax.broadcasted_iota(jnp.int32, (sub, 1), 0)
    cw = cw_ref[...]
    xc = xb * cw[3:4, :] + cb_ref[...]
    for s in (1, 2, 3):
        rolled = pltpu.roll(xb, s, axis=0)
        top = jnp.where(row8 < s, pltpu.roll(hist, s, axis=0), rolled[:sub, :])
        shifted = jnp.concatenate([top, rolled[sub:, :]], axis=0)
        xc = xc + shifted * cw[3 - s:4 - s, :]
    xcb = xc.astype(BF16)
    gate_r = _sigmoid(jnp.dot(xcb, wa_ref[...], preferred_element_type=F32) + ba_ref[...])
    gate_i = _sigmoid(jnp.dot(xcb, wx_ref[...], preferred_element_type=F32) + bx_ref[...])
    log_a = -LRU_C * gate_r * _softplus(-lam_ref[...])
    a_s[...] = jnp.exp(log_a)
    u_s[...] = jnp.sqrt(1.0 - jnp.exp(2.0 * log_a)) * gate_i * xc

    def group(j, h_prev):
        r0 = pl.multiple_of(j * sub, sub)
        a = a_s[pl.ds(r0, sub), :]
        u = u_s[pl.ds(r0, sub), :]
        for d in (1, 2, 4):
            keep = row8 >= d
            u = jnp.where(keep, a * pltpu.roll(u, d, axis=0) + u, u)
            a = jnp.where(keep, a * pltpu.roll(a, d, axis=0), a)
        h = u + a * h_prev
        u_s[pl.ds(r0, sub), :] = h
        return jnp.broadcast_to(h[sub - 1:sub, :], h.shape)

    h_last = lax.fori_loop(0, tt // sub, group, hlast_ref[...])
    hlast_ref[...] = h_last
    y_ref[...] = (u_s[...] * jax.nn.gelu(gb_ref[...])).astype(y_ref.dtype)


def _block_diag(w):
    nb, bi, bo = w.shape
    out = jnp.zeros((nb * bi, nb * bo), w.dtype)
    for i in range(nb):
        out = out.at[i * bi:(i + 1) * bi, i * bo:(i + 1) * bo].set(w[i])
    return out


def _lru(xb, gb, bsz, t, tt, conv_w, conv_b, wa, ba, wx, bx, lam):
    nt = t // tt
    width = LRU_WIDTH
    row = lambda x: x.reshape(1, -1)
    spec = pl.BlockSpec((tt, width), lambda b, i: (b * nt + i, 0))
    return pl.pallas_call(
        _lru_kernel,
        grid=(bsz, nt),
        in_specs=[spec, spec, _const_spec((4, width)), _const_spec((1, width)), _const_spec((width, width)),
                  _const_spec((1, width)), _const_spec((width, width)), _const_spec((1, width)),
                  _const_spec((1, width))],
        out_specs=spec,
        out_shape=jax.ShapeDtypeStruct((bsz * t, width), BF16),
        scratch_shapes=[pltpu.VMEM((SUBLANES, width), F32), pltpu.VMEM((SUBLANES, width), F32),
                        pltpu.VMEM((tt, width), F32), pltpu.VMEM((tt, width), F32)],
        compiler_params=_params("parallel", "arbitrary"),
        name="rglru",
    )(xb, gb, conv_w, row(conv_b), _block_diag(wa).astype(BF16), row(ba), _block_diag(wx).astype(BF16), row(bx),
      row(lam))


def _gla_kernel(q_ref, k_ref, v_ref, r_ref, gd_ref, gup_ref, gb_ref, gain_ref, tri_ref, y_ref, st_ref, cum_s):
    tt = q_ref.shape[0]
    sub = GLA_SUB

    @pl.when(pl.program_id(1) == 0)
    def _():
        st_ref[...] = jnp.zeros_like(st_ref)

    logits = jnp.dot(gd_ref[...].astype(BF16), gup_ref[...], preferred_element_type=F32) + gb_ref[...]
    log_a = -_softplus(-logits) * (1.0 / GLA_TAU)
    cum_s[...] = jnp.dot(tri_ref[...], log_a, preferred_element_type=F32, precision=lax.Precision.HIGHEST)
    t_i = lax.broadcasted_iota(jnp.int32, (sub, sub, 1), 0)
    s_i = lax.broadcasted_iota(jnp.int32, (sub, sub, 1), 1)
    causal = s_i <= t_i
    scale = GLA_DK ** -0.5
    contract_last = (((1,), (1,)), ((), ()))
    contract_first = (((0,), (0,)), ((), ()))

    def chunk(c, carry):
        r0 = pl.multiple_of(c * sub, sub)
        q = q_ref[pl.ds(r0, sub), :] * scale
        k = k_ref[pl.ds(r0, sub), :]
        v = v_ref[pl.ds(r0, sub), :]
        cum = cum_s[pl.ds(r0, sub), :]
        gate = gain_ref[...] * (lambda x: x * _sigmoid(x))(r_ref[pl.ds(r0, sub), :])
        for h in range(GLA_HEADS):
            ks = slice(h * GLA_DK, (h + 1) * GLA_DK)
            vs = slice(h * GLA_DV, (h + 1) * GLA_DV)
            qh, kh, ch, vh = q[:, ks], k[:, ks], cum[:, ks], v[:, vs]
            diff = ch[:, None, :] - ch[None, :, :]
            dec = jnp.where(causal, jnp.exp(jnp.minimum(diff, 0.0)), 0.0)
            sc = jnp.sum(qh[:, None, :] * kh[None, :, :] * dec, axis=-1, keepdims=True)
            intra = jnp.sum(sc * vh[None, :, :], axis=1)
            st = st_ref[h]
            inter = lax.dot_general((qh * jnp.exp(ch)).astype(BF16), st.astype(BF16), contract_last,
                                    preferred_element_type=F32)
            o = intra + inter
            last = ch[sub - 1:sub, :]
            khat = kh * jnp.exp(last - ch)
            st_ref[h] = st * jnp.exp(last) + lax.dot_general(vh.astype(BF16), khat.astype(BF16), contract_first,
                                                             preferred_element_type=F32)
            o = o * lax.rsqrt(jnp.mean(o * o, axis=-1, keepdims=True) + NORM_EPS)
            y_ref[pl.ds(r0, sub), vs] = (o * gate[:, vs]).astype(y_ref.dtype)
        return carry

    lax.fori_loop(0, tt // sub, chunk, 0)


def _gla(q, k, v, r, gd, bsz, t, tt, g_up, g_b, head_gain):
    nt = t // tt
    kw, vw = GLA_KEY_WIDTH, GLA_VAL_WIDTH
    gup = jnp.zeros((LANES, kw), F32).at[:GLA_LORA].set(g_up)
    i = jnp.arange(tt)
    tri = ((i[None, :] <= i[:, None]) & (i[None, :] // GLA_SUB == i[:, None] // GLA_SUB)).astype(F32)
    spec = lambda w: pl.BlockSpec((tt, w), lambda b, j: (b * nt + j, 0))
    return pl.pallas_call(
        _gla_kernel,
        grid=(bsz, nt),
        in_specs=[spec(kw), spec(kw), spec(vw), spec(vw), spec(LANES), _const_spec((LANES, kw)),
                  _const_spec((1, kw)), _const_spec((1, vw)), _const_spec((tt, tt))],
        out_specs=spec(vw),
        out_shape=jax.ShapeDtypeStruct((bsz * t, vw), BF16),
        scratch_shapes=[pltpu.VMEM((GLA_HEADS, GLA_DV, GLA_DK), F32), pltpu.VMEM((tt, kw), F32)],
        compiler_params=_params("parallel", "arbitrary"),
        name="gla",
    )(q, k, v, r, gd, gup.astype(BF16), g_b.reshape(1, -1), head_gain.reshape(1, -1), tri)


ROUTE_E1, ROUTE_E2, ROUTE_RANK1, ROUTE_RANK2, ROUTE_W1, ROUTE_W2 = range(6)


def _rms(x, gain):
    return x * lax.rsqrt(jnp.mean(x * x, axis=-1, keepdims=True) + NORM_EPS) * gain


def _moe_route_kernel(h_ref, gain_ref, wr_ref, br_ref, tri_ref, info_ref, cnt_ref, carry_s):
    @pl.when(pl.program_id(0) == 0)
    def _():
        carry_s[...] = jnp.zeros_like(carry_s)

    lane = lax.broadcasted_iota(jnp.int32, (1, LANES), 1)
    xn = _rms(h_ref[...], gain_ref[...])
    logits = jnp.dot(xn, wr_ref[...], preferred_element_type=F32, precision=lax.Precision.HIGHEST) + br_ref[...]
    neg = -jnp.inf
    far = 4 * LANES
    first = lambda hit: jnp.min(jnp.where(hit, lane, far), axis=-1, keepdims=True)
    is_group = lane < N_GROUPS
    gl = jnp.where(is_group, logits, neg)
    gmax = jnp.max(gl, axis=-1, keepdims=True)
    g_idx = first(gl == gmax)
    g_prob = 1.0 / jnp.sum(jnp.where(is_group, jnp.exp(logits - gmax), 0.0), axis=-1, keepdims=True)
    in_group = (lane >= N_GROUPS) & (lane < N_GROUPS + N_EXPERTS) & (((lane - N_GROUPS) >> 3) == g_idx)
    el = jnp.where(in_group, logits, neg)
    v1 = jnp.max(el, axis=-1, keepdims=True)
    i1 = first(el == v1)
    el2 = jnp.where(lane == i1, neg, el)
    v2 = jnp.max(el2, axis=-1, keepdims=True)
    i2 = first(el2 == v2)
    e21 = jnp.exp(v2 - v1)
    w1 = g_prob / (1.0 + e21)
    hit1 = lane == i1
    hit2 = lane == i2
    m = jnp.where(hit1 | hit2, 1.0, 0.0)
    before = jnp.dot(tri_ref[...], m.astype(BF16), preferred_element_type=F32) + carry_s[...]
    rank1 = jnp.sum(jnp.where(hit1, before, 0.0), axis=-1, keepdims=True)
    rank2 = jnp.sum(jnp.where(hit2, before, 0.0), axis=-1, keepdims=True)
    carry_s[...] += jnp.sum(m, axis=0, keepdims=True)
    cnt_ref[...] = carry_s[...]
    fields = ((i1 - N_GROUPS).astype(F32), (i2 - N_GROUPS).astype(F32), rank1, rank2, w1, w1 * e21)
    info = jnp.zeros(info_ref.shape, F32)
    for pos, val in enumerate(fields):
        info = jnp.where(lane == pos, val, info)
    info_ref[...] = info


def _row_copy(src, src_row, dst, dst_row, sem):
    return pltpu.make_async_copy(src.at[pl.ds(src_row, 1)], dst.at[pl.ds(dst_row, 1)], sem)


def _moe_dispatch_kernel(pos1_ref, pos2_ref, h_hbm, init_hbm, xs_hbm, sem):
    del init_hbm
    tm = pos1_ref.shape[-1]
    base = pl.program_id(0) * tm

    def copies(j):
        return (_row_copy(h_hbm, base + j, xs_hbm, pos1_ref[0, 0, j], sem.at[0]),
                _row_copy(h_hbm, base + j, xs_hbm, pos2_ref[0, 0, j], sem.at[1]))

    def start(j, c):
        for cp in copies(j):
            cp.start()
        return c

    def wait(j, c):
        for cp in copies(j):
            cp.wait()
        return c

    lax.fori_loop(0, tm, start, 0)
    lax.fori_loop(0, tm, wait, 0)


def _moe_ffn_kernel(te_ref, xs_ref, gain_ref, wg_ref, wu_ref, wd_ref, ys_ref):
    del te_ref
    xn = _rms(xs_ref[...], gain_ref[...]).astype(BF16)
    gate = jnp.dot(xn, wg_ref[0], preferred_element_type=F32)
    up = jnp.dot(xn, wu_ref[0], preferred_element_type=F32)
    hid = gate * _sigmoid(gate) * up
    ys_ref[...] = jnp.dot(hid.astype(BF16), wd_ref[0], preferred_element_type=F32)


def _moe_combine_kernel(pos1_ref, pos2_ref, h_ref, info_ref, ys_hbm, o_ref, buf1, buf2, sem):
    tm = h_ref.shape[0]

    def copies(j):
        return (_row_copy(ys_hbm, pos1_ref[0, 0, j], buf1, j, sem.at[0]),
                _row_copy(ys_hbm, pos2_ref[0, 0, j], buf2, j, sem.at[1]))

    def start(j, c):
        for cp in copies(j):
            cp.start()
        return c

    def wait(j, c):
        for cp in copies(j):
            cp.wait()
        return c

    lax.fori_loop(0, tm, start, 0)
    lax.fori_loop(0, tm, wait, 0)
    info = info_ref[...]
    w1 = info[:, ROUTE_W1:ROUTE_W1 + 1]
    w2 = info[:, ROUTE_W2:ROUTE_W2 + 1]
    o_ref[...] = h_ref[...] + w1 * buf1[...] + w2 * buf2[...]


def _moe(h, gain, w_group, b_group, w_router, b_router, w_gate, w_up, w_down, tm, tg):
    n, d = h.shape
    nt = n // tm
    wr = jnp.zeros((d, LANES), F32).at[:, :N_GROUPS].set(w_group).at[:, N_GROUPS:N_GROUPS + N_EXPERTS].set(w_router)
    br = jnp.zeros((1, LANES), F32).at[0, :N_GROUPS].set(b_group).at[0, N_GROUPS:N_GROUPS + N_EXPERTS].set(b_router)
    i = jnp.arange(tm)
    tri = (i[None, :] < i[:, None]).astype(BF16)
    info, cnt = pl.pallas_call(
        _moe_route_kernel,
        grid=(nt,),
        in_specs=[pl.BlockSpec((tm, d), lambda i: (i, 0)), _const_spec((1, d)), _const_spec((d, LANES)),
                  _const_spec((1, LANES)), _const_spec((tm, tm))],
        out_specs=[pl.BlockSpec((tm, LANES), lambda i: (i, 0)), _const_spec((1, LANES))],
        out_shape=[jax.ShapeDtypeStruct((n, LANES), F32), jax.ShapeDtypeStruct((1, LANES), F32)],
        scratch_shapes=[pltpu.VMEM((1, LANES), F32)],
        compiler_params=_params("arbitrary"),
        name="moe_route",
    )(h, gain.reshape(1, d), wr, br, tri)

    counts = cnt[0, N_GROUPS:N_GROUPS + N_EXPERTS].astype(jnp.int32)
    padded = (counts + tg - 1) // tg * tg
    ends = jnp.cumsum(padded)
    offs = ends - padded
    ints = info[:, :ROUTE_W1].astype(jnp.int32)
    pos1 = (offs[ints[:, ROUTE_E1]] + ints[:, ROUTE_RANK1]).reshape(nt, 1, tm)
    pos2 = (offs[ints[:, ROUTE_E2]] + ints[:, ROUTE_RANK2]).reshape(nt, 1, tm)
    n_rows = (2 * n + tg - 1) // tg * tg + N_EXPERTS * tg
    n_tiles = n_rows // tg
    tile_expert = jnp.minimum(jnp.searchsorted(ends, jnp.arange(n_tiles) * tg, side="right"),
                              N_EXPERTS - 1).astype(jnp.int32)

    pos_spec = pl.BlockSpec((1, 1, tm), lambda i: (i, 0, 0), memory_space=pltpu.SMEM)
    any_spec = pl.BlockSpec(memory_space=pl.ANY)
    xs = pl.pallas_call(
        _moe_dispatch_kernel,
        grid=(nt,),
        in_specs=[pos_spec, pos_spec, any_spec, any_spec],
        out_specs=any_spec,
        out_shape=jax.ShapeDtypeStruct((n_rows, d), F32),
        scratch_shapes=[pltpu.SemaphoreType.DMA((2,))],
        input_output_aliases={3: 0},
        compiler_params=_params("arbitrary"),
        name="moe_dispatch",
    )(pos1, pos2, h, jnp.zeros((n_rows, d), F32))

    ys = pl.pallas_call(
        _moe_ffn_kernel,
        grid_spec=pltpu.PrefetchScalarGridSpec(
            num_scalar_prefetch=1,
            grid=(n_tiles,),
            in_specs=[pl.BlockSpec((tg, d), lambda i, te: (i, 0)),
                      pl.BlockSpec((1, d), lambda i, te: (0, 0)),
                      pl.BlockSpec((1, d, EXPERT_FF), lambda i, te: (te[i], 0, 0)),
                      pl.BlockSpec((1, d, EXPERT_FF), lambda i, te: (te[i], 0, 0)),
                      pl.BlockSpec((1, EXPERT_FF, d), lambda i, te: (te[i], 0, 0))],
            out_specs=pl.BlockSpec((tg, d), lambda i, te: (i, 0))),
        out_shape=jax.ShapeDtypeStruct((n_rows, d), F32),
        compiler_params=_params("arbitrary"),
        name="moe_ffn",
    )(tile_expert, xs, gain.reshape(1, d), w_gate.astype(BF16), w_up.astype(BF16), w_down.astype(BF16))

    return pl.pallas_call(
        _moe_combine_kernel,
        grid=(nt,),
        in_specs=[pos_spec, pos_spec, pl.BlockSpec((tm, d), lambda i: (i, 0)),
                  pl.BlockSpec((tm, LANES), lambda i: (i, 0)), any_spec],
        out_specs=pl.BlockSpec((tm, d), lambda i: (i, 0)),
        out_shape=jax.ShapeDtypeStruct((n, d), F32),
        scratch_shapes=[pltpu.VMEM((tm, d), F32), pltpu.VMEM((tm, d), F32), pltpu.SemaphoreType.DMA((2,))],
        compiler_params=_params("arbitrary"),
        name="moe_combine",
    )(pos1, pos2, h, info, ys)


def _final_norm_kernel(x_ref, g_ref, o_ref):
    x = x_ref[...]
    o_ref[...] = x * lax.rsqrt(jnp.mean(x * x, axis=-1, keepdims=True) + NORM_EPS) * g_ref[...]


def _final_norm(h, gain, tm):
    n, d = h.shape
    return pl.pallas_call(
        _final_norm_kernel,
        grid=(n // tm,),
        in_specs=[pl.BlockSpec((tm, d), lambda i: (i, 0)), _const_spec((1, d))],
        out_specs=pl.BlockSpec((tm, d), lambda i: (i, 0)),
        out_shape=jax.ShapeDtypeStruct((n, d), F32),
        compiler_params=_params("parallel"),
        name="final_norm",
    )(h, gain.reshape(1, d))


def _even_mixer(h, bsz, t, tm, tt, ts, norm, w_in, mu, decay0, decay_up, iclr0, iclr_up, gate_up, k_k, k_a, r_k,
                ln_g, ln_b, conv_w, conv_b, lru_wa, lru_ba, lru_wx, lru_bx, lru_lambda, w_out):
    p, xb, gb = _norm_proj(h, norm, w_in, (RWKV_IN, LRU_WIDTH, LRU_WIDTH), tm)
    r, w, k, v, kk, b, g, bonus = _rwkv_prep(p, bsz, t, tt, mu, decay0, decay_up, iclr0, iclr_up, gate_up, k_k,
                                             k_a, r_k)
    o = _rwkv_scan(r, w, k, v, kk, b, bsz, t, ts)
    y_a = _rwkv_post(o, bonus, g, ln_g, ln_b, bsz, t, tt)
    y_b = _lru(xb, gb, bsz, t, tt, conv_w, conv_b, lru_wa, lru_ba, lru_wx, lru_bx, lru_lambda)
    return _out_proj(h, (y_a, y_b), (w_out[:RWKV_WIDTH], w_out[RWKV_WIDTH:]), tm)


def _odd_mixer(h, bsz, t, tm, tt, norm, w_in, g_up, g_b, head_gain, w_out):
    kw, vw = GLA_KEY_WIDTH, GLA_VAL_WIDTH
    d = w_in.shape[0]
    w_packed = jnp.concatenate(
        [w_in[:, :2 * kw + vw], w_in[:, 2 * kw + vw + GLA_LORA:], w_in[:, 2 * kw + vw:2 * kw + vw + GLA_LORA],
         jnp.zeros((d, LANES - GLA_LORA), w_in.dtype)], axis=1)
    q, k, v, r, gd = _norm_proj(h, norm, w_packed, (kw, kw, vw, vw, LANES), tm)
    y = _gla(q, k, v, r, gd, bsz, t, tt, g_up, g_b, head_gain)
    return _out_proj(h, (y,), (w_out,), tm)


def kernel(x, meta, e_norm, e_w_in, e_mu, e_decay0, e_decay_up, e_iclr0, e_iclr_up, e_gate_up, e_k_k, e_k_a, e_r_k, e_ln_g, e_ln_b, e_conv_w, e_conv_b, e_lru_wa, e_lru_ba, e_lru_wx, e_lru_bx, e_lru_lambda, e_w_out, o_norm, o_w_in, o_g_up, o_g_b, o_head_gain, o_w_out, m_norm, m_w_group, m_b_group, m_w_router, m_b_router, m_w_gate, m_w_up, m_w_down, final_norm):
    bsz, seq, d = x.shape
    t = seq + N_META
    n = bsz * t
    tm = _pick_tile(n, 384)
    tt = _pick_tile(t, 432)
    ts = _pick_tile(t, 48)
    depth = m_norm.shape[0]
    h = jnp.concatenate([jnp.broadcast_to(meta[None].astype(x.dtype), (bsz, N_META, d)), x], axis=1).reshape(n, d)
    for layer in range(depth):
        i = layer // 2
        if layer % 2 == 0:
            h = _even_mixer(h, bsz, t, tm, tt, ts, e_norm[i], e_w_in[i], e_mu[i], e_decay0[i], e_decay_up[i],
                            e_iclr0[i], e_iclr_up[i], e_gate_up[i], e_k_k[i], e_k_a[i], e_r_k[i], e_ln_g[i],
                            e_ln_b[i], e_conv_w[i], e_conv_b[i], e_lru_wa[i], e_lru_ba[i], e_lru_wx[i],
                            e_lru_bx[i], e_lru_lambda[i], e_w_out[i])
        else:
            h = _odd_mixer(h, bsz, t, tm, tt, o_norm[i], o_w_in[i], o_g_up[i], o_g_b[i], o_head_gain[i], o_w_out[i])
        h = _moe(h, m_norm[layer], m_w_group[layer], m_b_group[layer], m_w_router[layer], m_b_router[layer],
                 m_w_gate[layer], m_w_up[layer], m_w_down[layer], tm, MOE_TILE)
    out = _final_norm(h, final_norm, tm)
    return out.reshape(bsz, t, d)[:, N_META:]
```

```python
import functools

import jax
import jax.numpy as jnp
from jax import lax
from jax.experimental import pallas as pl
from jax.experimental.pallas import tpu as pltpu

F32 = jnp.float32
BF16 = jnp.bfloat16

N_META = 16
NORM_EPS = 1e-6
RWKV_HEADS = 8
HEAD_DIM = 64
RWKV_WIDTH = 512
LORA_W = 64
LORA_A = 64
LORA_G = 128
RWKV_IN = 3 * RWKV_WIDTH + LORA_W + LORA_A + LORA_G
GN_EPS = 64e-5
LRU_WIDTH = 512
LRU_C = 8.0
GLA_HEADS = 4
GLA_DK = 128
GLA_DV = 256
GLA_KEY_WIDTH = GLA_HEADS * GLA_DK
GLA_VAL_WIDTH = GLA_HEADS * GLA_DV
GLA_LORA = 16
GLA_TAU = 16.0
GLA_SUB = 16
N_GROUPS = 4
EXPERTS_PER_GROUP = 8
N_EXPERTS = N_GROUPS * EXPERTS_PER_GROUP
EXPERT_FF = 256
MOE_TILE = 256
DMA_UNROLL = 8

LANES = 128
SUBLANES = 8
VMEM_LIMIT = 48 * 1024 * 1024


def _pick_tile(n, target, mult=16):
    best = None
    for t in range(mult, min(n, target) + 1, mult):
        if n % t == 0:
            best = t
    assert best is not None, (n, target, mult)
    return best


def _params(*sem):
    return pltpu.CompilerParams(dimension_semantics=sem, vmem_limit_bytes=VMEM_LIMIT)


def _softplus(x):
    return jnp.maximum(x, 0.0) + jnp.log(1.0 + jnp.exp(-jnp.abs(x)))


def _sigmoid(x):
    return 1.0 / (1.0 + jnp.exp(-x))


def _block_ones(width, block, dtype):
    r = lax.broadcasted_iota(jnp.int32, (width, width), 0) // block
    c = lax.broadcasted_iota(jnp.int32, (width, width), 1) // block
    return (r == c).astype(dtype)


def _const_spec(shape):
    zeros = (0,) * len(shape)
    return pl.BlockSpec(shape, lambda *_: zeros)


def _norm_proj_kernel(x_ref, g_ref, w_ref, *o_refs):
    x = x_ref[...]
    y = x * lax.rsqrt(jnp.mean(x * x, axis=-1, keepdims=True) + NORM_EPS) * g_ref[...]
    y = y.astype(BF16)
    off = 0
    for o_ref in o_refs:
        width = o_ref.shape[-1]
        o_ref[...] = jnp.dot(y, w_ref[:, off:off + width], preferred_element_type=F32).astype(o_ref.dtype)
        off += width


def _norm_proj(x, gain, w, splits, tm):
    n, d = x.shape
    f = w.shape[1]
    assert sum(splits) == f and all(s % LANES == 0 for s in splits)
    return pl.pallas_call(
        _norm_proj_kernel,
        grid=(n // tm,),
        in_specs=[pl.BlockSpec((tm, d), lambda i: (i, 0)), _const_spec((1, d)), _const_spec((d, f))],
        out_specs=[pl.BlockSpec((tm, s), lambda i: (i, 0)) for s in splits],
        out_shape=[jax.ShapeDtypeStruct((n, s), F32) for s in splits],
        compiler_params=_params("parallel"),
        name="norm_proj",
    )(x, gain.reshape(1, d), w.astype(BF16))


def _out_proj_kernel(*refs):
    h_ref, o_ref = refs[0], refs[-1]
    pairs = refs[1:-1]
    acc = h_ref[...]
    for y_ref, w_ref in zip(pairs[0::2], pairs[1::2]):
        acc = acc + jnp.dot(y_ref[...], w_ref[...], preferred_element_type=F32)
    o_ref[...] = acc


def _out_proj(h, ys, ws, tm):
    n, d = h.shape
    in_specs = [pl.BlockSpec((tm, d), lambda i: (i, 0))]
    args = [h]
    for y, w in zip(ys, ws):
        in_specs += [pl.BlockSpec((tm, y.shape[1]), lambda i: (i, 0)), _const_spec(w.shape)]
        args += [y, w.astype(BF16)]
    return pl.pallas_call(
        _out_proj_kernel,
        grid=(n // tm,),
        in_specs=in_specs,
        out_specs=pl.BlockSpec((tm, d), lambda i: (i, 0)),
        out_shape=jax.ShapeDtypeStruct((n, d), F32),
        compiler_params=_params("parallel"),
        name="out_proj",
    )(*args)


def _rwkv_prep_kernel(p_ref, mu_ref, lora_ref, dec0_ref, iclr0_ref, gup_ref, kk_ref, ka_ref, rk_ref, ones_ref,
                      r_o, w_o, k_o, v_o, kkn_o, b_o, g_o, bonus_o, carry_ref):
    tt = p_ref.shape[0]
    width = RWKV_WIDTH

    @pl.when(pl.program_id(1) == 0)
    def _():
        carry_ref[...] = jnp.zeros_like(carry_ref)

    p = p_ref[...]
    row = lax.broadcasted_iota(jnp.int32, (tt, 1), 0)
    prev = jnp.where(row == 0, carry_ref[...], pltpu.roll(p, 1, axis=0))
    carry_ref[...] = p[tt - 1:tt, :]
    xs = p + mu_ref[...] * (prev - p)
    r = xs[:, 0:width]
    k = xs[:, width:2 * width]
    v = xs[:, 2 * width:3 * width]
    wa = xs[:, 3 * width:3 * width + LANES]
    gd = xs[:, 3 * width + LANES:]
    lane = lax.broadcasted_iota(jnp.int32, (1, LANES), 1)
    wa = jnp.where(lane < LORA_W, jnp.tanh(wa), wa)
    lora = jnp.dot(wa.astype(BF16), lora_ref[...], preferred_element_type=F32)
    wlog = -_softplus(-(dec0_ref[...] + lora[:, :width])) - 0.5
    decay = jnp.exp(-jnp.exp(wlog))
    a = _sigmoid(iclr0_ref[...] + lora[:, width:])
    g = jnp.dot(_sigmoid(gd).astype(BF16), gup_ref[...], preferred_element_type=F32)
    ones = ones_ref[...]
    kraw = k * kk_ref[...]
    ss = jnp.dot((kraw * kraw).astype(BF16), ones, preferred_element_type=F32)
    kkn = kraw / jnp.maximum(jnp.sqrt(ss), 1e-12)
    k_mod = k * (1.0 + (a - 1.0) * ka_ref[...])
    bonus = jnp.dot((r * k_mod * rk_ref[...]).astype(BF16), ones, preferred_element_type=F32) * v
    r_o[...] = r
    w_o[...] = decay
    k_o[...] = k_mod
    v_o[...] = v
    kkn_o[...] = kkn
    b_o[...] = kkn * a
    g_o[...] = g
    bonus_o[...] = bonus


def _rwkv_prep(p, bsz, t, tt, mu, decay0, decay_up, iclr0, iclr_up, gate_up, k_k, k_a, r_k):
    nt = t // tt
    width = RWKV_WIDTH
    lora_w = jnp.zeros((LANES, 2 * width), F32)
    lora_w = lora_w.at[:LORA_W, :width].set(decay_up).at[LORA_W:, width:].set(iclr_up)
    row = lambda x: x.reshape(1, -1)
    tm_spec = pl.BlockSpec((tt, width), lambda b, i: (i, b))
    outs = pl.pallas_call(
        _rwkv_prep_kernel,
        grid=(bsz, nt),
        in_specs=[pl.BlockSpec((tt, RWKV_IN), lambda b, i: (b * nt + i, 0)),
                  _const_spec((1, RWKV_IN)), _const_spec((LANES, 2 * width)), _const_spec((1, width)),
                  _const_spec((1, width)), _const_spec((LORA_G, width)), _const_spec((1, width)),
                  _const_spec((1, width)), _const_spec((1, width)), _const_spec((width, width))],
        out_specs=[tm_spec] * 8,
        out_shape=[jax.ShapeDtypeStruct((t, bsz * width), F32)] * 8,
        scratch_shapes=[pltpu.VMEM((1, RWKV_IN), F32)],
        compiler_params=_params("parallel", "arbitrary"),
        name="rwkv_prep",
    )(p, row(mu), lora_w.astype(BF16), row(decay0), row(iclr0), gate_up.astype(BF16), row(k_k), row(k_a),
      row(r_k), _block_ones(width, HEAD_DIM, BF16))
    return outs


SCAN_GROUP = LANES // RWKV_HEADS


def _rwkv_scan_kernel(r_ref, w_ref, k_ref, v_ref, kk_ref, b_ref, pick_ref, eye_ref, ones_ref, o_ref, s_ref, *, nb):
    ts = r_ref.shape[0]
    width = RWKV_WIDTH
    half = width // 2
    pack = 2 * SUBLANES
    groups = HEAD_DIM // pack

    @pl.when(pl.program_id(0) == 0)
    def _():
        s_ref[...] = jnp.zeros_like(s_ref)

    eye_bf = eye_ref[...]

    def row(ref, t, b):
        return ref[pl.ds(t, 1), b * width:(b + 1) * width]

    def times_row(x, r):
        r16 = jnp.broadcast_to(r, (pack, width)).astype(BF16)
        return (x.reshape(groups, pack, width) * r16[None]).reshape(HEAD_DIM, width)

    def head_sums(x, slot):
        return (jnp.dot(x[:, :half], pick_ref[slot, 0], preferred_element_type=F32)
                + jnp.dot(x[:, half:], pick_ref[slot, 1], preferred_element_type=F32))

    def head_bcast(x):
        ones = ones_ref[...]
        return jnp.concatenate([jnp.dot(x[:, :half], ones, preferred_element_type=F32),
                                jnp.dot(x[:, half:], ones, preferred_element_type=F32)], axis=1)

    def step(t, carry):
        bs = [slice(b * HEAD_DIM, (b + 1) * HEAD_DIM) for b in range(nb)]
        sa = head_bcast(jnp.concatenate(
            [times_row(s_ref[rs, :].astype(BF16), row(kk_ref, t, b)) for b, rs in enumerate(bs)], axis=0))
        vcol = head_bcast(jnp.concatenate([times_row(eye_bf, row(v_ref, t, b)) for b in range(nb)], axis=0))
        ys = []
        for b, rs in enumerate(bs):
            s = s_ref[rs, :] * row(w_ref, t, b) - sa[rs] * row(b_ref, t, b) + vcol[rs] * row(k_ref, t, b)
            s_ref[rs, :] = s
            ys.append(times_row(s.astype(BF16), row(r_ref, t, b)))
        slot = t % SCAN_GROUP
        g = t // SCAN_GROUP
        oc = head_sums(jnp.concatenate(ys, axis=0), slot)
        o_ref[g] = jnp.where(slot == 0, oc, o_ref[g] + oc)
        return carry

    lax.fori_loop(0, ts, step, 0, unroll=4)


def _rwkv_scan(r, w, k, v, kk, b, bsz, t, ts):
    width = RWKV_WIDTH
    half = width // 2
    assert ts % SCAN_GROUP == 0
    eye = jnp.tile(jnp.eye(HEAD_DIM, dtype=BF16), (1, RWKV_HEADS))
    ones = _block_ones(half, HEAD_DIM, BF16)
    head = (jnp.arange(half)[None, :, None] // HEAD_DIM + jnp.arange(2)[:, None, None] * (RWKV_HEADS // 2))
    col = jnp.arange(SCAN_GROUP)[:, None, None, None] * RWKV_HEADS + head[None]
    pick = (col == jnp.arange(LANES)[None, None, None, :]).astype(BF16)
    spec = pl.BlockSpec((ts, bsz * width), lambda i: (i, 0))
    rows = bsz * HEAD_DIM
    out = pl.pallas_call(
        functools.partial(_rwkv_scan_kernel, nb=bsz),
        grid=(t // ts,),
        in_specs=[spec] * 6 + [_const_spec(pick.shape), _const_spec(eye.shape), _const_spec(ones.shape)],
        out_specs=pl.BlockSpec((ts // SCAN_GROUP, rows, LANES), lambda i: (i, 0, 0)),
        out_shape=jax.ShapeDtypeStruct((t // SCAN_GROUP, rows, LANES), F32),
        scratch_shapes=[pltpu.VMEM((rows, width), F32)],
        compiler_params=_params("arbitrary"),
        name="rwkv_scan",
    )(r, w, k, v, kk, b, pick, eye, ones)
    out = out.reshape(t // SCAN_GROUP, bsz, HEAD_DIM, SCAN_GROUP, RWKV_HEADS)
    return out.transpose(1, 0, 3, 4, 2).reshape(bsz * t, width)


def _rwkv_post_kernel(o_ref, bonus_ref, g_ref, lng_ref, lnb_ref, ones_ref, y_ref):
    ones = ones_ref[...]

    def seg_mean(x):
        hi = x.astype(BF16)
        lo = (x - hi.astype(F32)).astype(BF16)
        return (jnp.dot(hi, ones, preferred_element_type=F32)
                + jnp.dot(lo, ones, preferred_element_type=F32)) * (1.0 / HEAD_DIM)

    o = o_ref[...]
    d = o - seg_mean(o)
    var = seg_mean(d * d)
    y = d * lax.rsqrt(var + GN_EPS) * lng_ref[...] + lnb_ref[...]
    y_ref[...] = ((y + bonus_ref[...]) * g_ref[...]).astype(y_ref.dtype)


def _rwkv_post(o, bonus, g, ln_g, ln_b, bsz, t, tt):
    nt = t // tt
    width = RWKV_WIDTH
    tm_spec = pl.BlockSpec((tt, width), lambda b, i: (i, b))
    tok_spec = pl.BlockSpec((tt, width), lambda b, i: (b * nt + i, 0))
    return pl.pallas_call(
        _rwkv_post_kernel,
        grid=(bsz, nt),
        in_specs=[tok_spec, tm_spec, tm_spec] + [_const_spec((1, width))] * 2 + [_const_spec((width, width))],
        out_specs=tok_spec,
        out_shape=jax.ShapeDtypeStruct((bsz * t, width), BF16),
        compiler_params=_params("parallel", "parallel"),
        name="rwkv_post",
    )(o, bonus, g, ln_g.reshape(1, -1), ln_b.reshape(1, -1), _block_ones(width, HEAD_DIM, BF16))


def _lru_kernel(xb_ref, gb_ref, cw_ref, cb_ref, wa_ref, ba_ref, wx_ref, bx_ref, lam_ref, y_ref,
                hist_ref, hlast_ref, a_s, u_s):
    tt = xb_ref.shape[0]
    sub = SUBLANES

    @pl.when(pl.program_id(1) == 0)
    def _():
        hist_ref[...] = jnp.zeros_like(hist_ref)
        hlast_ref[...] = jnp.zeros_like(hlast_ref)

    xb = xb_ref[...]
    hist = hist_ref[...]
    hist_ref[...] = xb[tt - sub:, :]
    row8 = lax.broadcasted_iota(jnp.int32, (sub, 1), 0)
    cw = cw_ref[...]
    xc = xb * cw[3:4, :] + cb_ref[...]
    for s in (1, 2, 3):
        rolled = pltpu.roll(xb, s, axis=0)
        top = jnp.where(row8 < s, pltpu.roll(hist, s, axis=0), rolled[:sub, :])
        shifted = jnp.concatenate([top, rolled[sub:, :]], axis=0)
        xc = xc + shifted * cw[3 - s:4 - s, :]
    xcb = xc.astype(BF16)
    gate_r = _sigmoid(jnp.dot(xcb, wa_ref[...], preferred_element_type=F32) + ba_ref[...])
    gate_i = _sigmoid(jnp.dot(xcb, wx_ref[...], preferred_element_type=F32) + bx_ref[...])
    log_a = -LRU_C * gate_r * _softplus(-lam_ref[...])
    a_s[...] = jnp.exp(log_a)
    u_s[...] = jnp.sqrt(1.0 - jnp.exp(2.0 * log_a)) * gate_i * xc

    def group(j, h_prev):
        r0 = pl.multiple_of(j * sub, sub)
        a = a_s[pl.ds(r0, sub), :]
        u = u_s[pl.ds(r0, sub), :]
        for d in (1, 2, 4):
            keep = row8 >= d
            u = jnp.where(keep, a * pltpu.roll(u, d, axis=0) + u, u)
            a = jnp.where(keep, a * pltpu.roll(a, d, axis=0), a)
        h = u + a * h_prev
        u_s[pl.ds(r0, sub), :] = h
        return jnp.broadcast_to(h[sub - 1:sub, :], h.shape)

    h_last = lax.fori_loop(0, tt // sub, group, hlast_ref[...])
    hlast_ref[...] = h_last
    y_ref[...] = (u_s[...] * jax.nn.gelu(gb_ref[...])).astype(y_ref.dtype)


def _block_diag(w):
    nb, bi, bo = w.shape
    out = jnp.zeros((nb * bi, nb * bo), w.dtype)
    for i in range(nb):
        out = out.at[i * bi:(i + 1) * bi, i * bo:(i + 1) * bo].set(w[i])
    return out


def _lru(xb, gb, bsz, t, tt, conv_w, conv_b, wa, ba, wx, bx, lam):
    nt = t // tt
    width = LRU_WIDTH
    row = lambda x: x.reshape(1, -1)
    spec = pl.BlockSpec((tt, width), lambda b, i: (b * nt + i, 0))
    return pl.pallas_call(
        _lru_kernel,
        grid=(bsz, nt),
        in_specs=[spec, spec, _const_spec((4, width)), _const_spec((1, width)), _const_spec((width, width)),
                  _const_spec((1, width)), _const_spec((width, width)), _const_spec((1, width)),
                  _const_spec((1, width))],
        out_specs=spec,
        out_shape=jax.ShapeDtypeStruct((bsz * t, width), BF16),
        scratch_shapes=[pltpu.VMEM((SUBLANES, width), F32), pltpu.VMEM((SUBLANES, width), F32),
                        pltpu.VMEM((tt, width), F32), pltpu.VMEM((tt, width), F32)],
        compiler_params=_params("parallel", "arbitrary"),
        name="rglru",
    )(xb, gb, conv_w, row(conv_b), _block_diag(wa).astype(BF16), row(ba), _block_diag(wx).astype(BF16), row(bx),
      row(lam))


def _gla_kernel(q_ref, k_ref, v_ref, r_ref, gd_ref, gup_ref, gb_ref, gain_ref, tri_ref, y_ref, st_ref, cum_s):
    tt = q_ref.shape[0]
    sub = GLA_SUB

    @pl.when(pl.program_id(1) == 0)
    def _():
        st_ref[...] = jnp.zeros_like(st_ref)

    logits = jnp.dot(gd_ref[...].astype(BF16), gup_ref[...], preferred_element_type=F32) + gb_ref[...]
    log_a = -_softplus(-logits) * (1.0 / GLA_TAU)
    cum_s[...] = jnp.dot(tri_ref[...], log_a, preferred_element_type=F32, precision=lax.Precision.HIGHEST)
    t_i = lax.broadcasted_iota(jnp.int32, (sub, sub, 1), 0)
    s_i = lax.broadcasted_iota(jnp.int32, (sub, sub, 1), 1)
    causal = s_i <= t_i
    scale = GLA_DK ** -0.5
    contract_last = (((1,), (1,)), ((), ()))
    contract_first = (((0,), (0,)), ((), ()))

    def chunk(c, carry):
        r0 = pl.multiple_of(c * sub, sub)
        q = q_ref[pl.ds(r0, sub), :] * scale
        k = k_ref[pl.ds(r0, sub), :]
        v = v_ref[pl.ds(r0, sub), :]
        cum = cum_s[pl.ds(r0, sub), :]
        gate = gain_ref[...] * (lambda x: x * _sigmoid(x))(r_ref[pl.ds(r0, sub), :])
        for h in range(GLA_HEADS):
            ks = slice(h * GLA_DK, (h + 1) * GLA_DK)
            vs = slice(h * GLA_DV, (h + 1) * GLA_DV)
            qh, kh, ch, vh = q[:, ks], k[:, ks], cum[:, ks], v[:, vs]
            diff = ch[:, None, :] - ch[None, :, :]
            dec = jnp.where(causal, jnp.exp(jnp.minimum(diff, 0.0)), 0.0)
            sc = jnp.sum(qh[:, None, :] * kh[None, :, :] * dec, axis=-1, keepdims=True)
            intra = jnp.sum(sc * vh[None, :, :], axis=1)
            st = st_ref[h]
            inter = lax.dot_general((qh * jnp.exp(ch)).astype(BF16), st.astype(BF16), contract_last,
                                    preferred_element_type=F32)
            o = intra + inter
            last = ch[sub - 1:sub, :]
            khat = kh * jnp.exp(last - ch)
            st_ref[h] = st * jnp.exp(last) + lax.dot_general(vh.astype(BF16), khat.astype(BF16), contract_first,
                                                             preferred_element_type=F32)
            o = o * lax.rsqrt(jnp.mean(o * o, axis=-1, keepdims=True) + NORM_EPS)
            y_ref[pl.ds(r0, sub), vs] = (o * gate[:, vs]).astype(y_ref.dtype)
        return carry

    lax.fori_loop(0, tt // sub, chunk, 0)


def _gla(q, k, v, r, gd, bsz, t, tt, g_up, g_b, head_gain):
    nt = t // tt
    kw, vw = GLA_KEY_WIDTH, GLA_VAL_WIDTH
    gup = jnp.zeros((LANES, kw), F32).at[:GLA_LORA].set(g_up)
    i = jnp.arange(tt)
    tri = ((i[None, :] <= i[:, None]) & (i[None, :] // GLA_SUB == i[:, None] // GLA_SUB)).astype(F32)
    spec = lambda w: pl.BlockSpec((tt, w), lambda b, j: (b * nt + j, 0))
    return pl.pallas_call(
        _gla_kernel,
        grid=(bsz, nt),
        in_specs=[spec(kw), spec(kw), spec(vw), spec(vw), spec(LANES), _const_spec((LANES, kw)),
                  _const_spec((1, kw)), _const_spec((1, vw)), _const_spec((tt, tt))],
        out_specs=spec(vw),
        out_shape=jax.ShapeDtypeStruct((bsz * t, vw), BF16),
        scratch_shapes=[pltpu.VMEM((GLA_HEADS, GLA_DV, GLA_DK), F32), pltpu.VMEM((tt, kw), F32)],
        compiler_params=_params("parallel", "arbitrary"),
        name="gla",
    )(q, k, v, r, gd, gup.astype(BF16), g_b.reshape(1, -1), head_gain.reshape(1, -1), tri)


ROUTE_E1, ROUTE_E2, ROUTE_RANK1, ROUTE_RANK2, ROUTE_W1, ROUTE_W2 = range(6)


def _rms(x, gain):
    return x * lax.rsqrt(jnp.mean(x * x, axis=-1, keepdims=True) + NORM_EPS) * gain


def _moe_route_kernel(h_ref, gain_ref, wr_ref, br_ref, tri_ref, info_ref, cnt_ref, carry_s):
    @pl.when(pl.program_id(0) == 0)
    def _():
        carry_s[...] = jnp.zeros_like(carry_s)

    lane = lax.broadcasted_iota(jnp.int32, (1, LANES), 1)
    xn = _rms(h_ref[...], gain_ref[...])
    logits = jnp.dot(xn, wr_ref[...], preferred_element_type=F32, precision=lax.Precision.HIGHEST) + br_ref[...]
    neg = -jnp.inf
    far = 4 * LANES
    first = lambda hit: jnp.min(jnp.where(hit, lane, far), axis=-1, keepdims=True)
    is_group = lane < N_GROUPS
    gl = jnp.where(is_group, logits, neg)
    gmax = jnp.max(gl, axis=-1, keepdims=True)
    g_idx = first(gl == gmax)
    g_prob = 1.0 / jnp.sum(jnp.where(is_group, jnp.exp(logits - gmax), 0.0), axis=-1, keepdims=True)
    in_group = (lane >= N_GROUPS) & (lane < N_GROUPS + N_EXPERTS) & (((lane - N_GROUPS) >> 3) == g_idx)
    el = jnp.where(in_group, logits, neg)
    v1 = jnp.max(el, axis=-1, keepdims=True)
    i1 = first(el == v1)
    el2 = jnp.where(lane == i1, neg, el)
    v2 = jnp.max(el2, axis=-1, keepdims=True)
    i2 = first(el2 == v2)
    e21 = jnp.exp(v2 - v1)
    w1 = g_prob / (1.0 + e21)
    hit1 = lane == i1
    hit2 = lane == i2
    m = jnp.where(hit1 | hit2, 1.0, 0.0)
    before = jnp.dot(tri_ref[...], m.astype(BF16), preferred_element_type=F32) + carry_s[...]
    rank1 = jnp.sum(jnp.where(hit1, before, 0.0), axis=-1, keepdims=True)
    rank2 = jnp.sum(jnp.where(hit2, before, 0.0), axis=-1, keepdims=True)
    carry_s[...] += jnp.sum(m, axis=0, keepdims=True)
    cnt_ref[...] = carry_s[...]
    fields = ((i1 - N_GROUPS).astype(F32), (i2 - N_GROUPS).astype(F32), rank1, rank2, w1, w1 * e21)
    info = jnp.zeros(info_ref.shape, F32)
    for pos, val in enumerate(fields):
        info = jnp.where(lane == pos, val, info)
    info_ref[...] = info


def _row_copy(src, src_row, dst, dst_row, sem):
    return pltpu.make_async_copy(src.at[pl.ds(src_row, 1)], dst.at[pl.ds(dst_row, 1)], sem)


def _start_then_wait(copies, n):
    def start(j, c):
        for cp in copies(j):
            cp.start()
        return c

    def wait(j, c):
        for cp in copies(j):
            cp.wait()
        return c

    lax.fori_loop(0, n, start, 0, unroll=DMA_UNROLL)
    lax.fori_loop(0, n, wait, 0, unroll=DMA_UNROLL)


def _moe_pos_kernel(info_ref, offs_ref, o_ref):
    info = info_ref[...]
    lane = lax.broadcasted_iota(jnp.int32, (1, LANES), 1)
    out = info
    for e_lane, rank_lane in ((ROUTE_E1, ROUTE_RANK1), (ROUTE_E2, ROUTE_RANK2)):
        e = info[:, e_lane:e_lane + 1].astype(jnp.int32)
        off = jnp.sum(jnp.where(lane == e, offs_ref[...], 0.0), axis=-1, keepdims=True)
        out = jnp.where(lane == rank_lane, info[:, rank_lane:rank_lane + 1] + off, out)
    o_ref[...] = out


def _moe_dispatch_kernel(pos1_ref, pos2_ref, h_ref, init_hbm, xs_hbm, sem):
    del init_hbm
    _start_then_wait(lambda j: (_row_copy(h_ref, j, xs_hbm, pos1_ref[0, 0, j], sem.at[0]),
                                _row_copy(h_ref, j, xs_hbm, pos2_ref[0, 0, j], sem.at[1])), h_ref.shape[0])


def _moe_ffn_kernel(te_ref, xs_ref, gain_ref, wg_ref, wu_ref, wd_ref, ys_ref):
    del te_ref
    xn = _rms(xs_ref[...], gain_ref[...]).astype(BF16)
    gate = jnp.dot(xn, wg_ref[0], preferred_element_type=F32)
    up = jnp.dot(xn, wu_ref[0], preferred_element_type=F32)
    hid = gate * _sigmoid(gate) * up
    ys_ref[...] = jnp.dot(hid.astype(BF16), wd_ref[0], preferred_element_type=F32)


def _moe_combine_kernel(pos1_ref, pos2_ref, h_ref, info_ref, ys_hbm, o_ref, buf1, buf2, sem):
    _start_then_wait(lambda j: (_row_copy(ys_hbm, pos1_ref[0, 0, j], buf1, j, sem.at[0]),
                                _row_copy(ys_hbm, pos2_ref[0, 0, j], buf2, j, sem.at[1])), h_ref.shape[0])
    info = info_ref[...]
    w1 = info[:, ROUTE_W1:ROUTE_W1 + 1]
    w2 = info[:, ROUTE_W2:ROUTE_W2 + 1]
    o_ref[...] = h_ref[...] + w1 * buf1[...] + w2 * buf2[...]


def _moe(h, gain, w_group, b_group, w_router, b_router, w_gate, w_up, w_down, tm, tg):
    n, d = h.shape
    nt = n // tm
    wr = jnp.zeros((d, LANES), F32).at[:, :N_GROUPS].set(w_group).at[:, N_GROUPS:N_GROUPS + N_EXPERTS].set(w_router)
    br = jnp.zeros((1, LANES), F32).at[0, :N_GROUPS].set(b_group).at[0, N_GROUPS:N_GROUPS + N_EXPERTS].set(b_router)
    i = jnp.arange(tm)
    tri = (i[None, :] < i[:, None]).astype(BF16)
    info, cnt = pl.pallas_call(
        _moe_route_kernel,
        grid=(nt,),
        in_specs=[pl.BlockSpec((tm, d), lambda i: (i, 0)), _const_spec((1, d)), _const_spec((d, LANES)),
                  _const_spec((1, LANES)), _const_spec((tm, tm))],
        out_specs=[pl.BlockSpec((tm, LANES), lambda i: (i, 0)), _const_spec((1, LANES))],
        out_shape=[jax.ShapeDtypeStruct((n, LANES), F32), jax.ShapeDtypeStruct((1, LANES), F32)],
        scratch_shapes=[pltpu.VMEM((1, LANES), F32)],
        compiler_params=_params("arbitrary"),
        name="moe_route",
    )(h, gain.reshape(1, d), wr, br, tri)

    counts = cnt[:, N_GROUPS:N_GROUPS + N_EXPERTS].astype(jnp.int32)
    padded = (counts + tg - 1) // tg * tg
    ends = jnp.cumsum(padded, axis=1)
    offs = jnp.zeros((1, LANES), F32).at[:, :N_EXPERTS].set((ends - padded).astype(F32))
    n_rows = (2 * n + tg - 1) // tg * tg + N_EXPERTS * tg
    n_tiles = n_rows // tg
    tile_expert = jnp.minimum(jnp.sum(jnp.arange(n_tiles)[:, None] * tg >= ends, axis=1),
                              N_EXPERTS - 1).astype(jnp.int32)
    info = pl.pallas_call(
        _moe_pos_kernel,
        grid=(nt,),
        in_specs=[pl.BlockSpec((tm, LANES), lambda i: (i, 0)), _const_spec((1, LANES))],
        out_specs=pl.BlockSpec((tm, LANES), lambda i: (i, 0)),
        out_shape=jax.ShapeDtypeStruct((n, LANES), F32),
        compiler_params=_params("parallel"),
        name="moe_pos",
    )(info, offs)
    pos1 = info[:, ROUTE_RANK1].astype(jnp.int32).reshape(nt, 1, tm)
    pos2 = info[:, ROUTE_RANK2].astype(jnp.int32).reshape(nt, 1, tm)

    pos_spec = pl.BlockSpec((1, 1, tm), lambda i: (i, 0, 0), memory_space=pltpu.SMEM)
    any_spec = pl.BlockSpec(memory_space=pl.ANY)
    xs = pl.pallas_call(
        _moe_dispatch_kernel,
        grid=(nt,),
        in_specs=[pos_spec, pos_spec, pl.BlockSpec((tm, d), lambda i: (i, 0)), any_spec],
        out_specs=any_spec,
        out_shape=jax.ShapeDtypeStruct((n_rows, d), F32),
        scratch_shapes=[pltpu.SemaphoreType.DMA((2,))],
        input_output_aliases={3: 0},
        compiler_params=_params("arbitrary"),
        name="moe_dispatch",
    )(pos1, pos2, h, jnp.zeros((n_rows, d), F32))

    ys = pl.pallas_call(
        _moe_ffn_kernel,
        grid_spec=pltpu.PrefetchScalarGridSpec(
            num_scalar_prefetch=1,
            grid=(n_tiles,),
            in_specs=[pl.BlockSpec((tg, d), lambda i, te: (i, 0)),
                      pl.BlockSpec((1, d), lambda i, te: (0, 0)),
                      pl.BlockSpec((1, d, EXPERT_FF), lambda i, te: (te[i], 0, 0)),
                      pl.BlockSpec((1, d, EXPERT_FF), lambda i, te: (te[i], 0, 0)),
                      pl.BlockSpec((1, EXPERT_FF, d), lambda i, te: (te[i], 0, 0))],
            out_specs=pl.BlockSpec((tg, d), lambda i, te: (i, 0))),
        out_shape=jax.ShapeDtypeStruct((n_rows, d), F32),
        compiler_params=_params("arbitrary"),
        name="moe_ffn",
    )(tile_expert, xs, gain.reshape(1, d), w_gate.astype(BF16), w_up.astype(BF16), w_down.astype(BF16))

    return pl.pallas_call(
        _moe_combine_kernel,
        grid=(nt,),
        in_specs=[pos_spec, pos_spec, pl.BlockSpec((tm, d), lambda i: (i, 0)),
                  pl.BlockSpec((tm, LANES), lambda i: (i, 0)), any_spec],
        out_specs=pl.BlockSpec((tm, d), lambda i: (i, 0)),
        out_shape=jax.ShapeDtypeStruct((n, d), F32),
        scratch_shapes=[pltpu.VMEM((tm, d), F32), pltpu.VMEM((tm, d), F32), pltpu.SemaphoreType.DMA((2,))],
        compiler_params=_params("arbitrary"),
        name="moe_combine",
    )(pos1, pos2, h, info, ys)


def _final_norm_kernel(x_ref, g_ref, o_ref):
    x = x_ref[...]
    o_ref[...] = x * lax.rsqrt(jnp.mean(x * x, axis=-1, keepdims=True) + NORM_EPS) * g_ref[...]


def _final_norm(h, gain, tm):
    n, d = h.shape
    return pl.pallas_call(
        _final_norm_kernel,
        grid=(n // tm,),
        in_specs=[pl.BlockSpec((tm, d), lambda i: (i, 0)), _const_spec((1, d))],
        out_specs=pl.BlockSpec((tm, d), lambda i: (i, 0)),
        out_shape=jax.ShapeDtypeStruct((n, d), F32),
        compiler_params=_params("parallel"),
        name="final_norm",
    )(h, gain.reshape(1, d))


def _even_mixer(h, bsz, t, tm, tt, ts, norm, w_in, mu, decay0, decay_up, iclr0, iclr_up, gate_up, k_k, k_a, r_k,
                ln_g, ln_b, conv_w, conv_b, lru_wa, lru_ba, lru_wx, lru_bx, lru_lambda, w_out):
    p, xb, gb = _norm_proj(h, norm, w_in, (RWKV_IN, LRU_WIDTH, LRU_WIDTH), tm)
    r, w, k, v, kk, b, g, bonus = _rwkv_prep(p, bsz, t, tt, mu, decay0, decay_up, iclr0, iclr_up, gate_up, k_k,
                                             k_a, r_k)
    o = _rwkv_scan(r, w, k, v, kk, b, bsz, t, ts)
    y_a = _rwkv_post(o, bonus, g, ln_g, ln_b, bsz, t, tt)
    y_b = _lru(xb, gb, bsz, t, tt, conv_w, conv_b, lru_wa, lru_ba, lru_wx, lru_bx, lru_lambda)
    return _out_proj(h, (y_a, y_b), (w_out[:RWKV_WIDTH], w_out[RWKV_WIDTH:]), tm)


def _odd_mixer(h, bsz, t, tm, tt, norm, w_in, g_up, g_b, head_gain, w_out):
    kw, vw = GLA_KEY_WIDTH, GLA_VAL_WIDTH
    d = w_in.shape[0]
    w_packed = jnp.concatenate(
        [w_in[:, :2 * kw + vw], w_in[:, 2 * kw + vw + GLA_LORA:], w_in[:, 2 * kw + vw:2 * kw + vw + GLA_LORA],
         jnp.zeros((d, LANES - GLA_LORA), w_in.dtype)], axis=1)
    q, k, v, r, gd = _norm_proj(h, norm, w_packed, (kw, kw, vw, vw, LANES), tm)
    y = _gla(q, k, v, r, gd, bsz, t, tt, g_up, g_b, head_gain)
    return _out_proj(h, (y,), (w_out,), tm)


def kernel(x, meta, e_norm, e_w_in, e_mu, e_decay0, e_decay_up, e_iclr0, e_iclr_up, e_gate_up, e_k_k, e_k_a, e_r_k, e_ln_g, e_ln_b, e_conv_w, e_conv_b, e_lru_wa, e_lru_ba, e_lru_wx, e_lru_bx, e_lru_lambda, e_w_out, o_norm, o_w_in, o_g_up, o_g_b, o_head_gain, o_w_out, m_norm, m_w_group, m_b_group, m_w_router, m_b_router, m_w_gate, m_w_up, m_w_down, final_norm):
    bsz, seq, d = x.shape
    t = seq + N_META
    n = bsz * t
    tm = _pick_tile(n, 384)
    tt = _pick_tile(t, 432)
    ts = _pick_tile(t, 48)
    depth = m_norm.shape[0]
    h = jnp.concatenate([jnp.broadcast_to(meta[None].astype(x.dtype), (bsz, N_META, d)), x], axis=1).reshape(n, d)
    for layer in range(depth):
        i = layer // 2
        if layer % 2 == 0:
            h = _even_mixer(h, bsz, t, tm, tt, ts, e_norm[i], e_w_in[i], e_mu[i], e_decay0[i], e_decay_up[i],
                            e_iclr0[i], e_iclr_up[i], e_gate_up[i], e_k_k[i], e_k_a[i], e_r_k[i], e_ln_g[i],
                            e_ln_b[i], e_conv_w[i], e_conv_b[i], e_lru_wa[i], e_lru_ba[i], e_lru_wx[i],
                            e_lru_bx[i], e_lru_lambda[i], e_w_out[i])
        else:
            h = _odd_mixer(h, bsz, t, tm, tt, o_norm[i], o_w_in[i], o_g_up[i], o_g_b[i], o_head_gain[i], o_w_out[i])
        h = _moe(h, m_norm[layer], m_w_group[layer], m_b_group[layer], m_w_router[layer], m_b_router[layer],
                 m_w_gate[layer], m_w_up[layer], m_w_down[layer], tm, MOE_TILE)
    out = _final_norm(h, final_norm, tm)
    return out.reshape(bsz, t, d)[:, N_META:]
```

```python
import functools

import jax
import jax.numpy as jnp
from jax import lax
from jax.experimental import pallas as pl
from jax.experimental.pallas import tpu as pltpu

F32 = jnp.float32
BF16 = jnp.bfloat16

N_META = 16
NORM_EPS = 1e-6
RWKV_HEADS = 8
HEAD_DIM = 64
RWKV_WIDTH = 512
LORA_W = 64
LORA_A = 64
LORA_G = 128
RWKV_IN = 3 * RWKV_WIDTH + LORA_W + LORA_A + LORA_G
GN_EPS = 64e-5
LRU_WIDTH = 512
LRU_C = 8.0
GLA_HEADS = 4
GLA_DK = 128
GLA_DV = 256
GLA_KEY_WIDTH = GLA_HEADS * GLA_DK
GLA_VAL_WIDTH = GLA_HEADS * GLA_DV
GLA_LORA = 16
GLA_TAU = 16.0
GLA_SUB = 16
N_GROUPS = 4
EXPERTS_PER_GROUP = 8
N_EXPERTS = N_GROUPS * EXPERTS_PER_GROUP
EXPERT_FF = 256
MOE_TILE = 256
DMA_UNROLL = 8

LANES = 128
SUBLANES = 8
VMEM_LIMIT = 48 * 1024 * 1024


def _pick_tile(n, target, mult=16):
    best = None
    for t in range(mult, min(n, target) + 1, mult):
        if n % t == 0:
            best = t
    assert best is not None, (n, target, mult)
    return best


def _params(*sem):
    return pltpu.CompilerParams(dimension_semantics=sem, vmem_limit_bytes=VMEM_LIMIT)


def _softplus(x):
    return jnp.maximum(x, 0.0) + jnp.log(1.0 + jnp.exp(-jnp.abs(x)))


def _sigmoid(x):
    return 1.0 / (1.0 + jnp.exp(-x))


def _block_ones(width, block, dtype):
    r = lax.broadcasted_iota(jnp.int32, (width, width), 0) // block
    c = lax.broadcasted_iota(jnp.int32, (width, width), 1) // block
    return (r == c).astype(dtype)


def _const_spec(shape):
    zeros = (0,) * len(shape)
    return pl.BlockSpec(shape, lambda *_: zeros)


def _norm_proj_kernel(x_ref, g_ref, w_ref, *o_refs):
    x = x_ref[...]
    y = x * lax.rsqrt(jnp.mean(x * x, axis=-1, keepdims=True) + NORM_EPS) * g_ref[...]
    y = y.astype(BF16)
    off = 0
    for o_ref in o_refs:
        width = o_ref.shape[-1]
        o_ref[...] = jnp.dot(y, w_ref[:, off:off + width], preferred_element_type=F32).astype(o_ref.dtype)
        off += width


def _norm_proj(x, gain, w, splits, tm):
    n, d = x.shape
    f = w.shape[1]
    assert sum(splits) == f and all(s % LANES == 0 for s in splits)
    return pl.pallas_call(
        _norm_proj_kernel,
        grid=(n // tm,),
        in_specs=[pl.BlockSpec((tm, d), lambda i: (i, 0)), _const_spec((1, d)), _const_spec((d, f))],
        out_specs=[pl.BlockSpec((tm, s), lambda i: (i, 0)) for s in splits],
        out_shape=[jax.ShapeDtypeStruct((n, s), F32) for s in splits],
        compiler_params=_params("parallel"),
        name="norm_proj",
    )(x, gain.reshape(1, d), w.astype(BF16))


def _out_proj_kernel(*refs):
    h_ref, o_ref = refs[0], refs[-1]
    pairs = refs[1:-1]
    acc = h_ref[...]
    for y_ref, w_ref in zip(pairs[0::2], pairs[1::2]):
        acc = acc + jnp.dot(y_ref[...], w_ref[...], preferred_element_type=F32)
    o_ref[...] = acc


def _out_proj(h, ys, ws, tm):
    n, d = h.shape
    in_specs = [pl.BlockSpec((tm, d), lambda i: (i, 0))]
    args = [h]
    for y, w in zip(ys, ws):
        in_specs += [pl.BlockSpec((tm, y.shape[1]), lambda i: (i, 0)), _const_spec(w.shape)]
        args += [y, w.astype(BF16)]
    return pl.pallas_call(
        _out_proj_kernel,
        grid=(n // tm,),
        in_specs=in_specs,
        out_specs=pl.BlockSpec((tm, d), lambda i: (i, 0)),
        out_shape=jax.ShapeDtypeStruct((n, d), F32),
        compiler_params=_params("parallel"),
        name="out_proj",
    )(*args)


def _rwkv_prep_kernel(p_ref, mu_ref, lora_ref, dec0_ref, iclr0_ref, gup_ref, kk_ref, ka_ref, rk_ref, ones_ref,
                      r_o, w_o, k_o, v_o, kkn_o, b_o, g_o, bonus_o, carry_ref):
    tt = p_ref.shape[0]
    width = RWKV_WIDTH

    @pl.when(pl.program_id(1) == 0)
    def _():
        carry_ref[...] = jnp.zeros_like(carry_ref)

    p = p_ref[...]
    row = lax.broadcasted_iota(jnp.int32, (tt, 1), 0)
    prev = jnp.where(row == 0, carry_ref[...], pltpu.roll(p, 1, axis=0))
    carry_ref[...] = p[tt - 1:tt, :]
    xs = p + mu_ref[...] * (prev - p)
    r = xs[:, 0:width]
    k = xs[:, width:2 * width]
    v = xs[:, 2 * width:3 * width]
    wa = xs[:, 3 * width:3 * width + LANES]
    gd = xs[:, 3 * width + LANES:]
    lane = lax.broadcasted_iota(jnp.int32, (1, LANES), 1)
    wa = jnp.where(lane < LORA_W, jnp.tanh(wa), wa)
    lora = jnp.dot(wa.astype(BF16), lora_ref[...], preferred_element_type=F32)
    wlog = -_softplus(-(dec0_ref[...] + lora[:, :width])) - 0.5
    decay = jnp.exp(-jnp.exp(wlog))
    a = _sigmoid(iclr0_ref[...] + lora[:, width:])
    g = jnp.dot(_sigmoid(gd).astype(BF16), gup_ref[...], preferred_element_type=F32)
    ones = ones_ref[...]
    kraw = k * kk_ref[...]
    ss = jnp.dot((kraw * kraw).astype(BF16), ones, preferred_element_type=F32)
    kkn = kraw / jnp.maximum(jnp.sqrt(ss), 1e-12)
    k_mod = k * (1.0 + (a - 1.0) * ka_ref[...])
    bonus = jnp.dot((r * k_mod * rk_ref[...]).astype(BF16), ones, preferred_element_type=F32) * v
    r_o[...] = r
    w_o[...] = decay
    k_o[...] = k_mod
    v_o[...] = v
    kkn_o[...] = kkn
    b_o[...] = kkn * a
    g_o[...] = g
    bonus_o[...] = bonus


def _rwkv_prep(p, bsz, t, tt, mu, decay0, decay_up, iclr0, iclr_up, gate_up, k_k, k_a, r_k):
    nt = t // tt
    width = RWKV_WIDTH
    lora_w = jnp.zeros((LANES, 2 * width), F32)
    lora_w = lora_w.at[:LORA_W, :width].set(decay_up).at[LORA_W:, width:].set(iclr_up)
    row = lambda x: x.reshape(1, -1)
    tm_spec = pl.BlockSpec((tt, width), lambda b, i: (i, b))
    outs = pl.pallas_call(
        _rwkv_prep_kernel,
        grid=(bsz, nt),
        in_specs=[pl.BlockSpec((tt, RWKV_IN), lambda b, i: (b * nt + i, 0)),
                  _const_spec((1, RWKV_IN)), _const_spec((LANES, 2 * width)), _const_spec((1, width)),
                  _const_spec((1, width)), _const_spec((LORA_G, width)), _const_spec((1, width)),
                  _const_spec((1, width)), _const_spec((1, width)), _const_spec((width, width))],
        out_specs=[tm_spec] * 8,
        out_shape=[jax.ShapeDtypeStruct((t, bsz * width), F32)] * 8,
        scratch_shapes=[pltpu.VMEM((1, RWKV_IN), F32)],
        compiler_params=_params("parallel", "arbitrary"),
        name="rwkv_prep",
    )(p, row(mu), lora_w.astype(BF16), row(decay0), row(iclr0), gate_up.astype(BF16), row(k_k), row(k_a),
      row(r_k), _block_ones(width, HEAD_DIM, BF16))
    return outs


SCAN_GROUP = LANES // RWKV_HEADS


def _rwkv_scan_kernel(r_ref, w_ref, k_ref, v_ref, kk_ref, b_ref, pick_ref, eye_ref, ones_ref, o_ref, s_ref, *, nb):
    ts = r_ref.shape[0]
    width = RWKV_WIDTH
    half = width // 2
    pack = 2 * SUBLANES
    groups = HEAD_DIM // pack

    @pl.when(pl.program_id(0) == 0)
    def _():
        s_ref[...] = jnp.zeros_like(s_ref)

    eye_bf = eye_ref[...]

    def row(ref, t, b):
        return ref[pl.ds(t, 1), b * width:(b + 1) * width]

    def times_row(x, r):
        r16 = jnp.broadcast_to(r, (pack, width)).astype(BF16)
        return (x.reshape(groups, pack, width) * r16[None]).reshape(HEAD_DIM, width)

    def head_sums(x, slot):
        return (jnp.dot(x[:, :half], pick_ref[slot, 0], preferred_element_type=F32)
                + jnp.dot(x[:, half:], pick_ref[slot, 1], preferred_element_type=F32))

    def head_bcast(x):
        ones = ones_ref[...]
        return jnp.concatenate([jnp.dot(x[:, :half], ones, preferred_element_type=F32),
                                jnp.dot(x[:, half:], ones, preferred_element_type=F32)], axis=1)

    def step(t, carry):
        bs = [slice(b * HEAD_DIM, (b + 1) * HEAD_DIM) for b in range(nb)]
        sa = head_bcast(jnp.concatenate(
            [times_row(s_ref[rs, :].astype(BF16), row(kk_ref, t, b)) for b, rs in enumerate(bs)], axis=0))
        vcol = head_bcast(jnp.concatenate([times_row(eye_bf, row(v_ref, t, b)) for b in range(nb)], axis=0))
        ys = []
        for b, rs in enumerate(bs):
            s = s_ref[rs, :] * row(w_ref, t, b) - sa[rs] * row(b_ref, t, b) + vcol[rs] * row(k_ref, t, b)
            s_ref[rs, :] = s
            ys.append(times_row(s.astype(BF16), row(r_ref, t, b)))
        slot = t % SCAN_GROUP
        g = t // SCAN_GROUP
        oc = head_sums(jnp.concatenate(ys, axis=0), slot)
        o_ref[g] = jnp.where(slot == 0, oc, o_ref[g] + oc)
        return carry

    lax.fori_loop(0, ts, step, 0, unroll=4)


def _rwkv_scan(r, w, k, v, kk, b, bsz, t, ts):
    width = RWKV_WIDTH
    half = width // 2
    assert ts % SCAN_GROUP == 0
    eye = jnp.tile(jnp.eye(HEAD_DIM, dtype=BF16), (1, RWKV_HEADS))
    ones = _block_ones(half, HEAD_DIM, BF16)
    head = (jnp.arange(half)[None, :, None] // HEAD_DIM + jnp.arange(2)[:, None, None] * (RWKV_HEADS // 2))
    col = jnp.arange(SCAN_GROUP)[:, None, None, None] * RWKV_HEADS + head[None]
    pick = (col == jnp.arange(LANES)[None, None, None, :]).astype(BF16)
    spec = pl.BlockSpec((ts, bsz * width), lambda i: (i, 0))
    rows = bsz * HEAD_DIM
    out = pl.pallas_call(
        functools.partial(_rwkv_scan_kernel, nb=bsz),
        grid=(t // ts,),
        in_specs=[spec] * 6 + [_const_spec(pick.shape), _const_spec(eye.shape), _const_spec(ones.shape)],
        out_specs=pl.BlockSpec((ts // SCAN_GROUP, rows, LANES), lambda i: (i, 0, 0)),
        out_shape=jax.ShapeDtypeStruct((t // SCAN_GROUP, rows, LANES), F32),
        scratch_shapes=[pltpu.VMEM((rows, width), F32)],
        compiler_params=_params("arbitrary"),
        name="rwkv_scan",
    )(r, w, k, v, kk, b, pick, eye, ones)
    out = out.reshape(t // SCAN_GROUP, bsz, HEAD_DIM, SCAN_GROUP, RWKV_HEADS)
    return out.transpose(1, 0, 3, 4, 2).reshape(bsz * t, width)


def _rwkv_post_kernel(o_ref, bonus_ref, g_ref, lng_ref, lnb_ref, ones_ref, y_ref):
    ones = ones_ref[...]

    def seg_mean(x):
        hi = x.astype(BF16)
        lo = (x - hi.astype(F32)).astype(BF16)
        return (jnp.dot(hi, ones, preferred_element_type=F32)
                + jnp.dot(lo, ones, preferred_element_type=F32)) * (1.0 / HEAD_DIM)

    o = o_ref[...]
    d = o - seg_mean(o)
    var = seg_mean(d * d)
    y = d * lax.rsqrt(var + GN_EPS) * lng_ref[...] + lnb_ref[...]
    y_ref[...] = ((y + bonus_ref[...]) * g_ref[...]).astype(y_ref.dtype)


def _rwkv_post(o, bonus, g, ln_g, ln_b, bsz, t, tt):
    nt = t // tt
    width = RWKV_WIDTH
    tm_spec = pl.BlockSpec((tt, width), lambda b, i: (i, b))
    tok_spec = pl.BlockSpec((tt, width), lambda b, i: (b * nt + i, 0))
    return pl.pallas_call(
        _rwkv_post_kernel,
        grid=(bsz, nt),
        in_specs=[tok_spec, tm_spec, tm_spec] + [_const_spec((1, width))] * 2 + [_const_spec((width, width))],
        out_specs=tok_spec,
        out_shape=jax.ShapeDtypeStruct((bsz * t, width), BF16),
        compiler_params=_params("parallel", "parallel"),
        name="rwkv_post",
    )(o, bonus, g, ln_g.reshape(1, -1), ln_b.reshape(1, -1), _block_ones(width, HEAD_DIM, BF16))


def _lru_kernel(xb_ref, gb_ref, cw_ref, cb_ref, wa_ref, ba_ref, wx_ref, bx_ref, lam_ref, y_ref,
                hist_ref, hlast_ref, a_s, u_s):
    tt = xb_ref.shape[0]
    sub = SUBLANES

    @pl.when(pl.program_id(1) == 0)
    def _():
        hist_ref[...] = jnp.zeros_like(hist_ref)
        hlast_ref[...] = jnp.zeros_like(hlast_ref)

    xb = xb_ref[...]
    hist = hist_ref[...]
    hist_ref[...] = xb[tt - sub:, :]
    row8 = lax.broadcasted_iota(jnp.int32, (sub, 1), 0)
    cw = cw_ref[...]
    xc = xb * cw[3:4, :] + cb_ref[...]
    for s in (1, 2, 3):
        rolled = pltpu.roll(xb, s, axis=0)
        top = jnp.where(row8 < s, pltpu.roll(hist, s, axis=0), rolled[:sub, :])
        shifted = jnp.concatenate([top, rolled[sub:, :]], axis=0)
        xc = xc + shifted * cw[3 - s:4 - s, :]
    xcb = xc.astype(BF16)
    gate_r = _sigmoid(jnp.dot(xcb, wa_ref[...], preferred_element_type=F32) + ba_ref[...])
    gate_i = _sigmoid(jnp.dot(xcb, wx_ref[...], preferred_element_type=F32) + bx_ref[...])
    log_a = -LRU_C * gate_r * _softplus(-lam_ref[...])
    a_s[...] = jnp.exp(log_a)
    u_s[...] = jnp.sqrt(1.0 - jnp.exp(2.0 * log_a)) * gate_i * xc

    def group(j, h_prev):
        r0 = pl.multiple_of(j * sub, sub)
        a = a_s[pl.ds(r0, sub), :]
        u = u_s[pl.ds(r0, sub), :]
        for d in (1, 2, 4):
            keep = row8 >= d
            u = jnp.where(keep, a * pltpu.roll(u, d, axis=0) + u, u)
            a = jnp.where(keep, a * pltpu.roll(a, d, axis=0), a)
        h = u + a * h_prev
        u_s[pl.ds(r0, sub), :] = h
        return jnp.broadcast_to(h[sub - 1:sub, :], h.shape)

    h_last = lax.fori_loop(0, tt // sub, group, hlast_ref[...])
    hlast_ref[...] = h_last
    y_ref[...] = (u_s[...] * jax.nn.gelu(gb_ref[...])).astype(y_ref.dtype)


def _block_diag(w):
    nb, bi, bo = w.shape
    out = jnp.zeros((nb * bi, nb * bo), w.dtype)
    for i in range(nb):
        out = out.at[i * bi:(i + 1) * bi, i * bo:(i + 1) * bo].set(w[i])
    return out


def _lru(xb, gb, bsz, t, tt, conv_w, conv_b, wa, ba, wx, bx, lam):
    nt = t // tt
    width = LRU_WIDTH
    row = lambda x: x.reshape(1, -1)
    spec = pl.BlockSpec((tt, width), lambda b, i: (b * nt + i, 0))
    return pl.pallas_call(
        _lru_kernel,
        grid=(bsz, nt),
        in_specs=[spec, spec, _const_spec((4, width)), _const_spec((1, width)), _const_spec((width, width)),
                  _const_spec((1, width)), _const_spec((width, width)), _const_spec((1, width)),
                  _const_spec((1, width))],
        out_specs=spec,
        out_shape=jax.ShapeDtypeStruct((bsz * t, width), BF16),
        scratch_shapes=[pltpu.VMEM((SUBLANES, width), F32), pltpu.VMEM((SUBLANES, width), F32),
                        pltpu.VMEM((tt, width), F32), pltpu.VMEM((tt, width), F32)],
        compiler_params=_params("parallel", "arbitrary"),
        name="rglru",
    )(xb, gb, conv_w, row(conv_b), _block_diag(wa).astype(BF16), row(ba), _block_diag(wx).astype(BF16), row(bx),
      row(lam))


GLA_FAST_LIMIT = 60.0


def _gla_kernel(q_ref, k_ref, v_ref, r_ref, gd_ref, gup_ref, gb_ref, gain_ref, tri_ref, y_ref, st_ref, cum_s, *,
                chunk):
    tt = q_ref.shape[0]
    sub = GLA_SUB

    @pl.when(pl.program_id(1) == 0)
    def _():
        st_ref[...] = jnp.zeros_like(st_ref)

    logits = jnp.dot(gd_ref[...].astype(BF16), gup_ref[...], preferred_element_type=F32) + gb_ref[...]
    log_a = -_softplus(-logits) * (1.0 / GLA_TAU)
    hi = log_a.astype(BF16)
    rest = log_a - hi.astype(F32)
    mid = rest.astype(BF16)
    lo = (rest - mid.astype(F32)).astype(BF16)
    tri = tri_ref[...]
    cum_s[...] = (jnp.dot(tri, hi, preferred_element_type=F32) + jnp.dot(tri, mid, preferred_element_type=F32)
                  + jnp.dot(tri, lo, preferred_element_type=F32))
    scale = GLA_DK ** -0.5
    contract_last = (((1,), (1,)), ((), ()))
    contract_first = (((0,), (0,)), ((), ()))
    heads = [(slice(h * GLA_DK, (h + 1) * GLA_DK), slice(h * GLA_DV, (h + 1) * GLA_DV)) for h in range(GLA_HEADS)]

    def finish(o, r0, rows, vs):
        o = o * lax.rsqrt(jnp.mean(o * o, axis=-1, keepdims=True) + NORM_EPS)
        rr = r_ref[pl.ds(r0, rows), vs]
        y_ref[pl.ds(r0, rows), vs] = (o * (gain_ref[:, vs] * rr * _sigmoid(rr))).astype(y_ref.dtype)

    def advance_state(h, kh, ch, vh, rows):
        last = ch[rows - 1:rows, :]
        khat = kh * jnp.exp(last - ch)
        st_ref[h] = st_ref[h] * jnp.exp(last) + lax.dot_general(
            vh.astype(BF16), khat.astype(BF16), contract_first, preferred_element_type=F32)

    def fast(r0):
        t_i = lax.broadcasted_iota(jnp.int32, (chunk, chunk), 0)
        s_i = lax.broadcasted_iota(jnp.int32, (chunk, chunk), 1)
        for h, (ks, vs) in enumerate(heads):
            qh = q_ref[pl.ds(r0, chunk), ks] * scale
            kh = k_ref[pl.ds(r0, chunk), ks]
            vh = v_ref[pl.ds(r0, chunk), vs]
            ch = cum_s[pl.ds(r0, chunk), ks]
            qd = (qh * jnp.exp(ch)).astype(BF16)
            kd = (kh * jnp.exp(-ch)).astype(BF16)
            sc = lax.dot_general(qd, kd, contract_last, preferred_element_type=F32)
            sc = jnp.where(s_i <= t_i, sc, 0.0).astype(BF16)
            o = jnp.dot(sc, vh.astype(BF16), preferred_element_type=F32) + lax.dot_general(
                qd, st_ref[h].astype(BF16), contract_last, preferred_element_type=F32)
            advance_state(h, kh, ch, vh, chunk)
            finish(o, r0, chunk, vs)

    def exact_block(r0, first):
        base = jnp.where(first, 0.0, cum_s[pl.ds(jnp.maximum(r0 - 1, 0), 1), :])
        t_i = lax.broadcasted_iota(jnp.int32, (sub, sub, 1), 0)
        s_i = lax.broadcasted_iota(jnp.int32, (sub, sub, 1), 1)
        for h, (ks, vs) in enumerate(heads):
            qh = q_ref[pl.ds(r0, sub), ks] * scale
            kh = k_ref[pl.ds(r0, sub), ks]
            vh = v_ref[pl.ds(r0, sub), vs]
            ch = cum_s[pl.ds(r0, sub), ks] - base[:, ks]
            diff = ch[:, None, :] - ch[None, :, :]
            dec = jnp.where(s_i <= t_i, jnp.exp(jnp.minimum(diff, 0.0)), 0.0)
            sc = jnp.sum(qh[:, None, :] * kh[None, :, :] * dec, axis=-1, keepdims=True)
            o = jnp.sum(sc * vh[None, :, :], axis=1) + lax.dot_general(
                (qh * jnp.exp(ch)).astype(BF16), st_ref[h].astype(BF16), contract_last, preferred_element_type=F32)
            advance_state(h, kh, ch, vh, sub)
            finish(o, r0, sub, vs)

    def per_chunk(c, carry):
        r0 = pl.multiple_of(c * chunk, sub)
        mild = jnp.min(cum_s[pl.ds(r0 + chunk - 1, 1), :]) >= -GLA_FAST_LIMIT

        @pl.when(mild)
        def _():
            fast(r0)

        @pl.when(jnp.logical_not(mild))
        def _():
            def block(i, carry2):
                exact_block(pl.multiple_of(r0 + i * sub, sub), i == 0)
                return carry2

            lax.fori_loop(0, chunk // sub, block, 0)

        return carry

    lax.fori_loop(0, tt // chunk, per_chunk, 0)


def _gla(q, k, v, r, gd, bsz, t, tt, g_up, g_b, head_gain):
    nt = t // tt
    kw, vw = GLA_KEY_WIDTH, GLA_VAL_WIDTH
    chunk = _pick_tile(tt, 144, GLA_SUB)
    gup = jnp.zeros((LANES, kw), F32).at[:GLA_LORA].set(g_up)
    i = jnp.arange(tt)
    tri = ((i[None, :] <= i[:, None]) & (i[None, :] // chunk == i[:, None] // chunk)).astype(BF16)
    spec = lambda w: pl.BlockSpec((tt, w), lambda b, j: (b * nt + j, 0))
    return pl.pallas_call(
        functools.partial(_gla_kernel, chunk=chunk),
        grid=(bsz, nt),
        in_specs=[spec(kw), spec(kw), spec(vw), spec(vw), spec(LANES), _const_spec((LANES, kw)),
                  _const_spec((1, kw)), _const_spec((1, vw)), _const_spec((tt, tt))],
        out_specs=spec(vw),
        out_shape=jax.ShapeDtypeStruct((bsz * t, vw), BF16),
        scratch_shapes=[pltpu.VMEM((GLA_HEADS, GLA_DV, GLA_DK), F32), pltpu.VMEM((tt, kw), F32)],
        compiler_params=_params("parallel", "arbitrary"),
        name="gla",
    )(q, k, v, r, gd, gup.astype(BF16), g_b.reshape(1, -1), head_gain.reshape(1, -1), tri)


ROUTE_E1, ROUTE_E2, ROUTE_RANK1, ROUTE_RANK2, ROUTE_W1, ROUTE_W2 = range(6)


def _rms(x, gain):
    return x * lax.rsqrt(jnp.mean(x * x, axis=-1, keepdims=True) + NORM_EPS) * gain


def _moe_route_kernel(h_ref, gain_ref, wr_ref, br_ref, tri_ref, info_ref, cnt_ref, carry_s):
    @pl.when(pl.program_id(0) == 0)
    def _():
        carry_s[...] = jnp.zeros_like(carry_s)

    lane = lax.broadcasted_iota(jnp.int32, (1, LANES), 1)
    xn = _rms(h_ref[...], gain_ref[...])
    logits = jnp.dot(xn, wr_ref[...], preferred_element_type=F32, precision=lax.Precision.HIGHEST) + br_ref[...]
    neg = -jnp.inf
    far = 4 * LANES
    first = lambda hit: jnp.min(jnp.where(hit, lane, far), axis=-1, keepdims=True)
    is_group = lane < N_GROUPS
    gl = jnp.where(is_group, logits, neg)
    gmax = jnp.max(gl, axis=-1, keepdims=True)
    g_idx = first(gl == gmax)
    g_prob = 1.0 / jnp.sum(jnp.where(is_group, jnp.exp(logits - gmax), 0.0), axis=-1, keepdims=True)
    in_group = (lane >= N_GROUPS) & (lane < N_GROUPS + N_EXPERTS) & (((lane - N_GROUPS) >> 3) == g_idx)
    el = jnp.where(in_group, logits, neg)
    v1 = jnp.max(el, axis=-1, keepdims=True)
    i1 = first(el == v1)
    el2 = jnp.where(lane == i1, neg, el)
    v2 = jnp.max(el2, axis=-1, keepdims=True)
    i2 = first(el2 == v2)
    e21 = jnp.exp(v2 - v1)
    w1 = g_prob / (1.0 + e21)
    hit1 = lane == i1
    hit2 = lane == i2
    m = jnp.where(hit1 | hit2, 1.0, 0.0)
    before = jnp.dot(tri_ref[...], m.astype(BF16), preferred_element_type=F32) + carry_s[...]
    rank1 = jnp.sum(jnp.where(hit1, before, 0.0), axis=-1, keepdims=True)
    rank2 = jnp.sum(jnp.where(hit2, before, 0.0), axis=-1, keepdims=True)
    carry_s[...] += jnp.sum(m, axis=0, keepdims=True)
    cnt_ref[...] = carry_s[...]
    fields = ((i1 - N_GROUPS).astype(F32), (i2 - N_GROUPS).astype(F32), rank1, rank2, w1, w1 * e21)
    info = jnp.zeros(info_ref.shape, F32)
    for pos, val in enumerate(fields):
        info = jnp.where(lane == pos, val, info)
    info_ref[...] = info


def _row_copy(src, src_row, dst, dst_row, sem):
    return pltpu.make_async_copy(src.at[pl.ds(src_row, 1)], dst.at[pl.ds(dst_row, 1)], sem)


def _start_then_wait(copies, n):
    def start(j, c):
        for cp in copies(j):
            cp.start()
        return c

    def wait(j, c):
        for cp in copies(j):
            cp.wait()
        return c

    lax.fori_loop(0, n, start, 0, unroll=DMA_UNROLL)
    lax.fori_loop(0, n, wait, 0, unroll=DMA_UNROLL)


def _moe_pos_kernel(info_ref, offs_ref, o_ref):
    info = info_ref[...]
    lane = lax.broadcasted_iota(jnp.int32, (1, LANES), 1)
    out = info
    for e_lane, rank_lane in ((ROUTE_E1, ROUTE_RANK1), (ROUTE_E2, ROUTE_RANK2)):
        e = info[:, e_lane:e_lane + 1].astype(jnp.int32)
        off = jnp.sum(jnp.where(lane == e, offs_ref[...], 0.0), axis=-1, keepdims=True)
        out = jnp.where(lane == rank_lane, info[:, rank_lane:rank_lane + 1] + off, out)
    o_ref[...] = out


def _moe_dispatch_kernel(pos1_ref, pos2_ref, h_ref, init_hbm, xs_hbm, sem):
    del init_hbm
    _start_then_wait(lambda j: (_row_copy(h_ref, j, xs_hbm, pos1_ref[0, 0, j], sem.at[0]),
                                _row_copy(h_ref, j, xs_hbm, pos2_ref[0, 0, j], sem.at[1])), h_ref.shape[0])


def _moe_ffn_kernel(te_ref, xs_ref, gain_ref, wg_ref, wu_ref, wd_ref, ys_ref):
    del te_ref
    xn = _rms(xs_ref[...], gain_ref[...]).astype(BF16)
    gate = jnp.dot(xn, wg_ref[0], preferred_element_type=F32)
    up = jnp.dot(xn, wu_ref[0], preferred_element_type=F32)
    hid = gate * _sigmoid(gate) * up
    ys_ref[...] = jnp.dot(hid.astype(BF16), wd_ref[0], preferred_element_type=F32)


def _moe_combine_kernel(pos1_ref, pos2_ref, h_ref, info_ref, ys_hbm, o_ref, buf1, buf2, sem):
    _start_then_wait(lambda j: (_row_copy(ys_hbm, pos1_ref[0, 0, j], buf1, j, sem.at[0]),
                                _row_copy(ys_hbm, pos2_ref[0, 0, j], buf2, j, sem.at[1])), h_ref.shape[0])
    info = info_ref[...]
    w1 = info[:, ROUTE_W1:ROUTE_W1 + 1]
    w2 = info[:, ROUTE_W2:ROUTE_W2 + 1]
    o_ref[...] = h_ref[...] + w1 * buf1[...] + w2 * buf2[...]


def _moe(h, gain, w_group, b_group, w_router, b_router, w_gate, w_up, w_down, tm, tg):
    n, d = h.shape
    nt = n // tm
    wr = jnp.zeros((d, LANES), F32).at[:, :N_GROUPS].set(w_group).at[:, N_GROUPS:N_GROUPS + N_EXPERTS].set(w_router)
    br = jnp.zeros((1, LANES), F32).at[0, :N_GROUPS].set(b_group).at[0, N_GROUPS:N_GROUPS + N_EXPERTS].set(b_router)
    i = jnp.arange(tm)
    tri = (i[None, :] < i[:, None]).astype(BF16)
    info, cnt = pl.pallas_call(
        _moe_route_kernel,
        grid=(nt,),
        in_specs=[pl.BlockSpec((tm, d), lambda i: (i, 0)), _const_spec((1, d)), _const_spec((d, LANES)),
                  _const_spec((1, LANES)), _const_spec((tm, tm))],
        out_specs=[pl.BlockSpec((tm, LANES), lambda i: (i, 0)), _const_spec((1, LANES))],
        out_shape=[jax.ShapeDtypeStruct((n, LANES), F32), jax.ShapeDtypeStruct((1, LANES), F32)],
        scratch_shapes=[pltpu.VMEM((1, LANES), F32)],
        compiler_params=_params("arbitrary"),
        name="moe_route",
    )(h, gain.reshape(1, d), wr, br, tri)

    counts = cnt[:, N_GROUPS:N_GROUPS + N_EXPERTS].astype(jnp.int32)
    padded = (counts + tg - 1) // tg * tg
    ends = jnp.cumsum(padded, axis=1)
    offs = jnp.zeros((1, LANES), F32).at[:, :N_EXPERTS].set((ends - padded).astype(F32))
    n_rows = (2 * n + tg - 1) // tg * tg + N_EXPERTS * tg
    n_tiles = n_rows // tg
    tile_expert = jnp.minimum(jnp.sum(jnp.arange(n_tiles)[:, None] * tg >= ends, axis=1),
                              N_EXPERTS - 1).astype(jnp.int32)
    info = pl.pallas_call(
        _moe_pos_kernel,
        grid=(nt,),
        in_specs=[pl.BlockSpec((tm, LANES), lambda i: (i, 0)), _const_spec((1, LANES))],
        out_specs=pl.BlockSpec((tm, LANES), lambda i: (i, 0)),
        out_shape=jax.ShapeDtypeStruct((n, LANES), F32),
        compiler_params=_params("parallel"),
        name="moe_pos",
    )(info, offs)
    pos1 = info[:, ROUTE_RANK1].astype(jnp.int32).reshape(nt, 1, tm)
    pos2 = info[:, ROUTE_RANK2].astype(jnp.int32).reshape(nt, 1, tm)

    pos_spec = pl.BlockSpec((1, 1, tm), lambda i: (i, 0, 0), memory_space=pltpu.SMEM)
    any_spec = pl.BlockSpec(memory_space=pl.ANY)
    xs = pl.pallas_call(
        _moe_dispatch_kernel,
        grid=(nt,),
        in_specs=[pos_spec, pos_spec, pl.BlockSpec((tm, d), lambda i: (i, 0)), any_spec],
        out_specs=any_spec,
        out_shape=jax.ShapeDtypeStruct((n_rows, d), F32),
        scratch_shapes=[pltpu.SemaphoreType.DMA((2,))],
        input_output_aliases={3: 0},
        compiler_params=_params("arbitrary"),
        name="moe_dispatch",
    )(pos1, pos2, h, jnp.zeros((n_rows, d), F32))

    ys = pl.pallas_call(
        _moe_ffn_kernel,
        grid_spec=pltpu.PrefetchScalarGridSpec(
            num_scalar_prefetch=1,
            grid=(n_tiles,),
            in_specs=[pl.BlockSpec((tg, d), lambda i, te: (i, 0)),
                      pl.BlockSpec((1, d), lambda i, te: (0, 0)),
                      pl.BlockSpec((1, d, EXPERT_FF), lambda i, te: (te[i], 0, 0)),
                      pl.BlockSpec((1, d, EXPERT_FF), lambda i, te: (te[i], 0, 0)),
                      pl.BlockSpec((1, EXPERT_FF, d), lambda i, te: (te[i], 0, 0))],
            out_specs=pl.BlockSpec((tg, d), lambda i, te: (i, 0))),
        out_shape=jax.ShapeDtypeStruct((n_rows, d), F32),
        compiler_params=_params("arbitrary"),
        name="moe_ffn",
    )(tile_expert, xs, gain.reshape(1, d), w_gate.astype(BF16), w_up.astype(BF16), w_down.astype(BF16))

    return pl.pallas_call(
        _moe_combine_kernel,
        grid=(nt,),
        in_specs=[pos_spec, pos_spec, pl.BlockSpec((tm, d), lambda i: (i, 0)),
                  pl.BlockSpec((tm, LANES), lambda i: (i, 0)), any_spec],
        out_specs=pl.BlockSpec((tm, d), lambda i: (i, 0)),
        out_shape=jax.ShapeDtypeStruct((n, d), F32),
        scratch_shapes=[pltpu.VMEM((tm, d), F32), pltpu.VMEM((tm, d), F32), pltpu.SemaphoreType.DMA((2,))],
        compiler_params=_params("arbitrary"),
        name="moe_combine",
    )(pos1, pos2, h, info, ys)


def _final_norm_kernel(x_ref, g_ref, o_ref):
    x = x_ref[...]
    o_ref[...] = x * lax.rsqrt(jnp.mean(x * x, axis=-1, keepdims=True) + NORM_EPS) * g_ref[...]


def _final_norm(h, gain, tm):
    n, d = h.shape
    return pl.pallas_call(
        _final_norm_kernel,
        grid=(n // tm,),
        in_specs=[pl.BlockSpec((tm, d), lambda i: (i, 0)), _const_spec((1, d))],
        out_specs=pl.BlockSpec((tm, d), lambda i: (i, 0)),
        out_shape=jax.ShapeDtypeStruct((n, d), F32),
        compiler_params=_params("parallel"),
        name="final_norm",
    )(h, gain.reshape(1, d))


def _even_mixer(h, bsz, t, tm, tt, ts, norm, w_in, mu, decay0, decay_up, iclr0, iclr_up, gate_up, k_k, k_a, r_k,
                ln_g, ln_b, conv_w, conv_b, lru_wa, lru_ba, lru_wx, lru_bx, lru_lambda, w_out):
    p, xb, gb = _norm_proj(h, norm, w_in, (RWKV_IN, LRU_WIDTH, LRU_WIDTH), tm)
    r, w, k, v, kk, b, g, bonus = _rwkv_prep(p, bsz, t, tt, mu, decay0, decay_up, iclr0, iclr_up, gate_up, k_k,
                                             k_a, r_k)
    o = _rwkv_scan(r, w, k, v, kk, b, bsz, t, ts)
    y_a = _rwkv_post(o, bonus, g, ln_g, ln_b, bsz, t, tt)
    y_b = _lru(xb, gb, bsz, t, tt, conv_w, conv_b, lru_wa, lru_ba, lru_wx, lru_bx, lru_lambda)
    return _out_proj(h, (y_a, y_b), (w_out[:RWKV_WIDTH], w_out[RWKV_WIDTH:]), tm)


def _odd_mixer(h, bsz, t, tm, tt, norm, w_in, g_up, g_b, head_gain, w_out):
    kw, vw = GLA_KEY_WIDTH, GLA_VAL_WIDTH
    d = w_in.shape[0]
    w_packed = jnp.concatenate(
        [w_in[:, :2 * kw + vw], w_in[:, 2 * kw + vw + GLA_LORA:], w_in[:, 2 * kw + vw:2 * kw + vw + GLA_LORA],
         jnp.zeros((d, LANES - GLA_LORA), w_in.dtype)], axis=1)
    q, k, v, r, gd = _norm_proj(h, norm, w_packed, (kw, kw, vw, vw, LANES), tm)
    y = _gla(q, k, v, r, gd, bsz, t, tt, g_up, g_b, head_gain)
    return _out_proj(h, (y,), (w_out,), tm)


def kernel(x, meta, e_norm, e_w_in, e_mu, e_decay0, e_decay_up, e_iclr0, e_iclr_up, e_gate_up, e_k_k, e_k_a, e_r_k, e_ln_g, e_ln_b, e_conv_w, e_conv_b, e_lru_wa, e_lru_ba, e_lru_wx, e_lru_bx, e_lru_lambda, e_w_out, o_norm, o_w_in, o_g_up, o_g_b, o_head_gain, o_w_out, m_norm, m_w_group, m_b_group, m_w_router, m_b_router, m_w_gate, m_w_up, m_w_down, final_norm):
    bsz, seq, d = x.shape
    t = seq + N_META
    n = bsz * t
    tm = _pick_tile(n, 384)
    tt = _pick_tile(t, 432)
    ts = _pick_tile(t, 48)
    depth = m_norm.shape[0]
    h = jnp.concatenate([jnp.broadcast_to(meta[None].astype(x.dtype), (bsz, N_META, d)), x], axis=1).reshape(n, d)
    for layer in range(depth):
        i = layer // 2
        if layer % 2 == 0:
            h = _even_mixer(h, bsz, t, tm, tt, ts, e_norm[i], e_w_in[i], e_mu[i], e_decay0[i], e_decay_up[i],
                            e_iclr0[i], e_iclr_up[i], e_gate_up[i], e_k_k[i], e_k_a[i], e_r_k[i], e_ln_g[i],
                            e_ln_b[i], e_conv_w[i], e_conv_b[i], e_lru_wa[i], e_lru_ba[i], e_lru_wx[i],
                            e_lru_bx[i], e_lru_lambda[i], e_w_out[i])
        else:
            h = _odd_mixer(h, bsz, t, tm, tt, o_norm[i], o_w_in[i], o_g_up[i], o_g_b[i], o_head_gain[i], o_w_out[i])
        h = _moe(h, m_norm[layer], m_w_group[layer], m_b_group[layer], m_w_router[layer], m_b_router[layer],
                 m_w_gate[layer], m_w_up[layer], m_w_down[layer], tm, MOE_TILE)
    out = _final_norm(h, final_norm, tm)
    return out.reshape(bsz, t, d)[:, N_META:]
```

```python
import functools

import jax
import jax.numpy as jnp
from jax import lax
from jax.experimental import pallas as pl
from jax.experimental.pallas import tpu as pltpu

F32 = jnp.float32
BF16 = jnp.bfloat16

N_META = 16
NORM_EPS = 1e-6
RWKV_HEADS = 8
HEAD_DIM = 64
RWKV_WIDTH = 512
LORA_W = 64
LORA_A = 64
LORA_G = 128
RWKV_IN = 3 * RWKV_WIDTH + LORA_W + LORA_A + LORA_G
GN_EPS = 64e-5
LRU_WIDTH = 512
LRU_C = 8.0
GLA_HEADS = 4
GLA_DK = 128
GLA_DV = 256
GLA_KEY_WIDTH = GLA_HEADS * GLA_DK
GLA_VAL_WIDTH = GLA_HEADS * GLA_DV
GLA_LORA = 16
GLA_TAU = 16.0
GLA_SUB = 16
N_GROUPS = 4
EXPERTS_PER_GROUP = 8
N_EXPERTS = N_GROUPS * EXPERTS_PER_GROUP
EXPERT_FF = 256
MOE_TILE = 256
DMA_UNROLL = 8

LANES = 128
SUBLANES = 8
VMEM_LIMIT = 48 * 1024 * 1024


def _pick_tile(n, target, mult=16):
    best = None
    for t in range(mult, min(n, target) + 1, mult):
        if n % t == 0:
            best = t
    assert best is not None, (n, target, mult)
    return best


def _params(*sem):
    return pltpu.CompilerParams(dimension_semantics=sem, vmem_limit_bytes=VMEM_LIMIT)


def _softplus(x):
    return jnp.maximum(x, 0.0) + jnp.log(1.0 + jnp.exp(-jnp.abs(x)))


def _sigmoid(x):
    return 1.0 / (1.0 + jnp.exp(-x))


def _block_ones(width, block, dtype):
    r = lax.broadcasted_iota(jnp.int32, (width, width), 0) // block
    c = lax.broadcasted_iota(jnp.int32, (width, width), 1) // block
    return (r == c).astype(dtype)


def _const_spec(shape):
    zeros = (0,) * len(shape)
    return pl.BlockSpec(shape, lambda *_: zeros)


def _norm_proj_kernel(x_ref, g_ref, w_ref, *o_refs):
    x = x_ref[...]
    y = x * lax.rsqrt(jnp.mean(x * x, axis=-1, keepdims=True) + NORM_EPS) * g_ref[...]
    y = y.astype(BF16)
    off = 0
    for o_ref in o_refs:
        width = o_ref.shape[-1]
        o_ref[...] = jnp.dot(y, w_ref[:, off:off + width], preferred_element_type=F32).astype(o_ref.dtype)
        off += width


def _norm_proj(x, gain, w, splits, tm):
    n, d = x.shape
    f = w.shape[1]
    assert sum(splits) == f and all(s % LANES == 0 for s in splits)
    return pl.pallas_call(
        _norm_proj_kernel,
        grid=(n // tm,),
        in_specs=[pl.BlockSpec((tm, d), lambda i: (i, 0)), _const_spec((1, d)), _const_spec((d, f))],
        out_specs=[pl.BlockSpec((tm, s), lambda i: (i, 0)) for s in splits],
        out_shape=[jax.ShapeDtypeStruct((n, s), F32) for s in splits],
        compiler_params=_params("parallel"),
        name="norm_proj",
    )(x, gain.reshape(1, d), w.astype(BF16))


def _out_proj_kernel(*refs):
    h_ref, o_ref = refs[0], refs[-1]
    pairs = refs[1:-1]
    acc = h_ref[...]
    for y_ref, w_ref in zip(pairs[0::2], pairs[1::2]):
        acc = acc + jnp.dot(y_ref[...], w_ref[...], preferred_element_type=F32)
    o_ref[...] = acc


def _out_proj(h, ys, ws, tm):
    n, d = h.shape
    in_specs = [pl.BlockSpec((tm, d), lambda i: (i, 0))]
    args = [h]
    for y, w in zip(ys, ws):
        in_specs += [pl.BlockSpec((tm, y.shape[1]), lambda i: (i, 0)), _const_spec(w.shape)]
        args += [y, w.astype(BF16)]
    return pl.pallas_call(
        _out_proj_kernel,
        grid=(n // tm,),
        in_specs=in_specs,
        out_specs=pl.BlockSpec((tm, d), lambda i: (i, 0)),
        out_shape=jax.ShapeDtypeStruct((n, d), F32),
        compiler_params=_params("parallel"),
        name="out_proj",
    )(*args)


def _rwkv_prep_kernel(p_ref, mu_ref, lora_ref, dec0_ref, iclr0_ref, gup_ref, kk_ref, ka_ref, rk_ref, ones_ref,
                      r_o, w_o, k_o, v_o, kkn_o, b_o, g_o, bonus_o, carry_ref):
    tt = p_ref.shape[0]
    width = RWKV_WIDTH

    @pl.when(pl.program_id(1) == 0)
    def _():
        carry_ref[...] = jnp.zeros_like(carry_ref)

    p = p_ref[...]
    row = lax.broadcasted_iota(jnp.int32, (tt, 1), 0)
    prev = jnp.where(row == 0, carry_ref[...], pltpu.roll(p, 1, axis=0))
    carry_ref[...] = p[tt - 1:tt, :]
    xs = p + mu_ref[...] * (prev - p)
    r = xs[:, 0:width]
    k = xs[:, width:2 * width]
    v = xs[:, 2 * width:3 * width]
    wa = xs[:, 3 * width:3 * width + LANES]
    gd = xs[:, 3 * width + LANES:]
    lane = lax.broadcasted_iota(jnp.int32, (1, LANES), 1)
    wa = jnp.where(lane < LORA_W, jnp.tanh(wa), wa)
    lora = jnp.dot(wa.astype(BF16), lora_ref[...], preferred_element_type=F32)
    wlog = -_softplus(-(dec0_ref[...] + lora[:, :width])) - 0.5
    a = _sigmoid(iclr0_ref[...] + lora[:, width:])
    g = jnp.dot(_sigmoid(gd).astype(BF16), gup_ref[...], preferred_element_type=F32)
    ones = ones_ref[...]
    kraw = k * kk_ref[...]
    ss = jnp.dot((kraw * kraw).astype(BF16), ones, preferred_element_type=F32)
    kkn = kraw / jnp.maximum(jnp.sqrt(ss), 1e-12)
    k_mod = k * (1.0 + (a - 1.0) * ka_ref[...])
    bonus = jnp.dot((r * k_mod * rk_ref[...]).astype(BF16), ones, preferred_element_type=F32) * v
    r_o[...] = r
    w_o[...] = -jnp.exp(wlog)
    k_o[...] = k_mod
    v_o[...] = v
    kkn_o[...] = kkn
    b_o[...] = kkn * a
    g_o[...] = g
    bonus_o[...] = bonus


def _rwkv_prep(p, bsz, t, tt, mu, decay0, decay_up, iclr0, iclr_up, gate_up, k_k, k_a, r_k):
    nt = t // tt
    width = RWKV_WIDTH
    lora_w = jnp.zeros((LANES, 2 * width), F32)
    lora_w = lora_w.at[:LORA_W, :width].set(decay_up).at[LORA_W:, width:].set(iclr_up)
    row = lambda x: x.reshape(1, -1)
    tok_spec = pl.BlockSpec((tt, width), lambda b, i: (b * nt + i, 0))
    outs = pl.pallas_call(
        _rwkv_prep_kernel,
        grid=(bsz, nt),
        in_specs=[pl.BlockSpec((tt, RWKV_IN), lambda b, i: (b * nt + i, 0)),
                  _const_spec((1, RWKV_IN)), _const_spec((LANES, 2 * width)), _const_spec((1, width)),
                  _const_spec((1, width)), _const_spec((LORA_G, width)), _const_spec((1, width)),
                  _const_spec((1, width)), _const_spec((1, width)), _const_spec((width, width))],
        out_specs=[tok_spec] * 8,
        out_shape=[jax.ShapeDtypeStruct((bsz * t, width), F32)] * 8,
        scratch_shapes=[pltpu.VMEM((1, RWKV_IN), F32)],
        compiler_params=_params("parallel", "arbitrary"),
        name="rwkv_prep",
    )(p, row(mu), lora_w.astype(BF16), row(decay0), row(iclr0), gate_up.astype(BF16), row(k_k), row(k_a),
      row(r_k), _block_ones(width, HEAD_DIM, BF16))
    return outs


RWKV_CHUNK = 48


def _rwkv_chunk_kernel(r_ref, lw_ref, k_ref, v_ref, kk_ref, b_ref, tri_ref, o_ref, st_ref, *, chunk):
    streams, tt, _ = r_ref.shape
    ln = chunk
    pairs = streams * (RWKV_WIDTH // LANES)

    @pl.when(pl.program_id(1) == 0)
    def _():
        st_ref[...] = jnp.zeros_like(st_ref)

    tri = tri_ref[...]
    head0 = lax.broadcasted_iota(jnp.int32, (1, LANES), 1) < HEAD_DIM
    t_i = lax.broadcasted_iota(jnp.int32, (ln, 2 * ln), 0)
    col = lax.broadcasted_iota(jnp.int32, (ln, 2 * ln), 1)
    s_i = jnp.where(col < ln, col, col - ln)
    strict = s_i < t_i
    incl = s_i <= t_i
    same_head = ((lax.broadcasted_iota(jnp.int32, (LANES, LANES), 0) < HEAD_DIM)
                 == (lax.broadcasted_iota(jnp.int32, (LANES, LANES), 1) < HEAD_DIM))
    contract_last = (((1,), (1,)), ((), ()))
    contract_first = (((0,), (0,)), ((), ()))

    def by_head(z):
        return jnp.concatenate([jnp.where(head0, z, 0.0), jnp.where(head0, 0.0, z)], axis=0).astype(BF16)

    def per_chunk(ci, carry):
        r0 = pl.multiple_of(ci * ln, SUBLANES)
        rows = pl.ds(r0, ln)
        e_c, e_prev, e_inv, e_rest = [], [], [], []
        for i in range(streams):
            lw = lw_ref[i, rows, :]
            hi = lw.astype(BF16)
            rest = lw - hi.astype(F32)
            mid = rest.astype(BF16)
            lo = (rest - mid.astype(F32)).astype(BF16)
            c = (jnp.dot(tri, hi, preferred_element_type=F32) + jnp.dot(tri, mid, preferred_element_type=F32)
                 + jnp.dot(tri, lo, preferred_element_type=F32))
            e_c.append(jnp.exp(c))
            e_prev.append(jnp.exp(c - lw))
            e_inv.append(jnp.exp(-c))
            e_rest.append(jnp.exp(c[ln - 1:ln, :] - c))
        prs = range(pairs)
        per = RWKV_WIDTH // LANES
        lanes = [(p // per, slice(p % per * LANES, (p % per + 1) * LANES)) for p in prs]
        kt, rt, sb, sk, npair, st, stb, vblk, u = ([None] * pairs for _ in range(9))
        for p, (i, ps) in enumerate(lanes):
            kt[p] = kk_ref[i, rows, ps] * e_prev[i][:, ps]
            rt[p] = r_ref[i, rows, ps] * e_c[i][:, ps]
            x = jnp.concatenate([kt[p], rt[p]], axis=0).astype(BF16)
            sb[p] = lax.dot_general(x, by_head(b_ref[i, rows, ps] * e_inv[i][:, ps]), contract_last,
                                    preferred_element_type=F32)
            sk[p] = lax.dot_general(x, by_head(k_ref[i, rows, ps] * e_inv[i][:, ps]), contract_last,
                                    preferred_element_type=F32)
            npair[p] = jnp.where(strict, sb[p][:ln], 0.0)
            st[p] = st_ref[p]
            stb[p] = st[p].astype(BF16)
            vblk[p] = by_head(v_ref[i, rows, ps])
            u[p] = (lax.dot_general(kt[p].astype(BF16), stb[p], contract_last, preferred_element_type=F32)
                    + jnp.dot(jnp.where(strict, sk[p][:ln], 0.0).astype(BF16), vblk[p], preferred_element_type=F32))
        solved = [[] for _ in prs]
        for j in range(ln // SUBLANES):
            lo8 = j * SUBLANES
            blk = [u[p][:SUBLANES] for p in prs]
            for s in range(SUBLANES - 1):
                for p in prs:
                    nj = npair[p][lo8:lo8 + SUBLANES]
                    coef = jnp.where(head0, nj[:, lo8 + s:lo8 + s + 1], nj[:, ln + lo8 + s:ln + lo8 + s + 1])
                    blk[p] = blk[p] - coef * blk[p][s:s + 1]
            for p in prs:
                solved[p].append(blk[p])
                if lo8 + SUBLANES < ln:
                    zero = jnp.zeros((SUBLANES, LANES), F32)
                    pieces = [zero] * (2 * ln // SUBLANES)
                    pieces[j] = jnp.where(head0, blk[p], 0.0)
                    pieces[ln // SUBLANES + j] = jnp.where(head0, 0.0, blk[p])
                    u[p] = u[p][SUBLANES:] - jnp.dot(npair[p][lo8 + SUBLANES:].astype(BF16),
                                                     jnp.concatenate(pieces, axis=0).astype(BF16),
                                                     preferred_element_type=F32)
        for p, (i, ps) in enumerate(lanes):
            up = jnp.concatenate(solved[p], axis=0)
            o = (lax.dot_general(rt[p].astype(BF16), stb[p], contract_last, preferred_element_type=F32)
                 + jnp.dot(jnp.where(incl, sk[p][ln:], 0.0).astype(BF16), vblk[p], preferred_element_type=F32)
                 - jnp.dot(jnp.where(incl, sb[p][ln:], 0.0).astype(BF16), by_head(up), preferred_element_type=F32))
            o_ref[i, rows, ps] = o
            grow = (lax.dot_general(v_ref[i, rows, ps].astype(BF16), (k_ref[i, rows, ps] * e_rest[i][:, ps]).astype(BF16),
                                    contract_first, preferred_element_type=F32)
                    - lax.dot_general(up.astype(BF16), (b_ref[i, rows, ps] * e_rest[i][:, ps]).astype(BF16), contract_first,
                                      preferred_element_type=F32))
            st_ref[p] = st[p] * e_c[i][ln - 1:ln, ps] + jnp.where(same_head, grow, 0.0)
        return carry

    lax.fori_loop(0, tt // ln, per_chunk, 0)


def _rwkv_chunked(r, lw, k, v, kk, b, bsz, t, tt):
    width = RWKV_WIDTH
    nt = t // tt
    chunk = RWKV_CHUNK
    assert tt % chunk == 0
    streams = 2 if bsz % 2 == 0 else 1
    nb = bsz // streams
    i = jnp.arange(chunk)
    tri = (i[None, :] <= i[:, None]).astype(BF16)
    spec = pl.BlockSpec((streams, tt, width), lambda bb, j: (0, bb * nt + j, 0))
    out = pl.pallas_call(
        functools.partial(_rwkv_chunk_kernel, chunk=chunk),
        grid=(nb, nt),
        in_specs=[spec] * 6 + [_const_spec((chunk, chunk))],
        out_specs=spec,
        out_shape=jax.ShapeDtypeStruct((streams, nb * t, width), F32),
        scratch_shapes=[pltpu.VMEM((streams * (width // LANES), LANES, LANES), F32)],
        compiler_params=_params("parallel", "arbitrary"),
        name="rwkv_chunked",
    )(*(x.reshape(streams, nb * t, width) for x in (r, lw, k, v, kk, b)), tri)
    return out.reshape(bsz * t, width)


def _rwkv_post_kernel(o_ref, bonus_ref, g_ref, lng_ref, lnb_ref, ones_ref, y_ref):
    ones = ones_ref[...]

    def seg_mean(x):
        hi = x.astype(BF16)
        lo = (x - hi.astype(F32)).astype(BF16)
        return (jnp.dot(hi, ones, preferred_element_type=F32)
                + jnp.dot(lo, ones, preferred_element_type=F32)) * (1.0 / HEAD_DIM)

    o = o_ref[...]
    d = o - seg_mean(o)
    var = seg_mean(d * d)
    y = d * lax.rsqrt(var + GN_EPS) * lng_ref[...] + lnb_ref[...]
    y_ref[...] = ((y + bonus_ref[...]) * g_ref[...]).astype(y_ref.dtype)


def _rwkv_post(o, bonus, g, ln_g, ln_b, bsz, t, tt):
    nt = t // tt
    width = RWKV_WIDTH
    tok_spec = pl.BlockSpec((tt, width), lambda b, i: (b * nt + i, 0))
    return pl.pallas_call(
        _rwkv_post_kernel,
        grid=(bsz, nt),
        in_specs=[tok_spec] * 3 + [_const_spec((1, width))] * 2 + [_const_spec((width, width))],
        out_specs=tok_spec,
        out_shape=jax.ShapeDtypeStruct((bsz * t, width), BF16),
        compiler_params=_params("parallel", "parallel"),
        name="rwkv_post",
    )(o, bonus, g, ln_g.reshape(1, -1), ln_b.reshape(1, -1), _block_ones(width, HEAD_DIM, BF16))


def _lru_kernel(xb_ref, gb_ref, cw_ref, cb_ref, wa_ref, ba_ref, wx_ref, bx_ref, lam_ref, y_ref,
                hist_ref, hlast_ref, a_s, u_s):
    tt = xb_ref.shape[0]
    sub = SUBLANES

    @pl.when(pl.program_id(1) == 0)
    def _():
        hist_ref[...] = jnp.zeros_like(hist_ref)
        hlast_ref[...] = jnp.zeros_like(hlast_ref)

    xb = xb_ref[...]
    hist = hist_ref[...]
    hist_ref[...] = xb[tt - sub:, :]
    row8 = lax.broadcasted_iota(jnp.int32, (sub, 1), 0)
    cw = cw_ref[...]
    xc = xb * cw[3:4, :] + cb_ref[...]
    for s in (1, 2, 3):
        rolled = pltpu.roll(xb, s, axis=0)
        top = jnp.where(row8 < s, pltpu.roll(hist, s, axis=0), rolled[:sub, :])
        shifted = jnp.concatenate([top, rolled[sub:, :]], axis=0)
        xc = xc + shifted * cw[3 - s:4 - s, :]
    xcb = xc.astype(BF16)
    gate_r = _sigmoid(jnp.dot(xcb, wa_ref[...], preferred_element_type=F32) + ba_ref[...])
    gate_i = _sigmoid(jnp.dot(xcb, wx_ref[...], preferred_element_type=F32) + bx_ref[...])
    log_a = -LRU_C * gate_r * _softplus(-lam_ref[...])
    a_s[...] = jnp.exp(log_a)
    u_s[...] = jnp.sqrt(1.0 - jnp.exp(2.0 * log_a)) * gate_i * xc

    def group(j, h_prev):
        r0 = pl.multiple_of(j * sub, sub)
        a = a_s[pl.ds(r0, sub), :]
        u = u_s[pl.ds(r0, sub), :]
        for d in (1, 2, 4):
            keep = row8 >= d
            u = jnp.where(keep, a * pltpu.roll(u, d, axis=0) + u, u)
            a = jnp.where(keep, a * pltpu.roll(a, d, axis=0), a)
        h = u + a * h_prev
        u_s[pl.ds(r0, sub), :] = h
        return jnp.broadcast_to(h[sub - 1:sub, :], h.shape)

    h_last = lax.fori_loop(0, tt // sub, group, hlast_ref[...])
    hlast_ref[...] = h_last
    y_ref[...] = (u_s[...] * jax.nn.gelu(gb_ref[...])).astype(y_ref.dtype)


def _block_diag(w):
    nb, bi, bo = w.shape
    out = jnp.zeros((nb * bi, nb * bo), w.dtype)
    for i in range(nb):
        out = out.at[i * bi:(i + 1) * bi, i * bo:(i + 1) * bo].set(w[i])
    return out


def _lru(xb, gb, bsz, t, tt, conv_w, conv_b, wa, ba, wx, bx, lam):
    nt = t // tt
    width = LRU_WIDTH
    row = lambda x: x.reshape(1, -1)
    spec = pl.BlockSpec((tt, width), lambda b, i: (b * nt + i, 0))
    return pl.pallas_call(
        _lru_kernel,
        grid=(bsz, nt),
        in_specs=[spec, spec, _const_spec((4, width)), _const_spec((1, width)), _const_spec((width, width)),
                  _const_spec((1, width)), _const_spec((width, width)), _const_spec((1, width)),
                  _const_spec((1, width))],
        out_specs=spec,
        out_shape=jax.ShapeDtypeStruct((bsz * t, width), BF16),
        scratch_shapes=[pltpu.VMEM((SUBLANES, width), F32), pltpu.VMEM((SUBLANES, width), F32),
                        pltpu.VMEM((tt, width), F32), pltpu.VMEM((tt, width), F32)],
        compiler_params=_params("parallel", "arbitrary"),
        name="rglru",
    )(xb, gb, conv_w, row(conv_b), _block_diag(wa).astype(BF16), row(ba), _block_diag(wx).astype(BF16), row(bx),
      row(lam))


GLA_FAST_LIMIT = 60.0


def _gla_kernel(q_ref, k_ref, v_ref, r_ref, gd_ref, gup_ref, gb_ref, gain_ref, tri_ref, y_ref, st_ref, cum_s, *,
                chunk):
    tt = q_ref.shape[0]
    sub = GLA_SUB

    @pl.when(pl.program_id(1) == 0)
    def _():
        st_ref[...] = jnp.zeros_like(st_ref)

    logits = jnp.dot(gd_ref[...].astype(BF16), gup_ref[...], preferred_element_type=F32) + gb_ref[...]
    log_a = -_softplus(-logits) * (1.0 / GLA_TAU)
    hi = log_a.astype(BF16)
    rest = log_a - hi.astype(F32)
    mid = rest.astype(BF16)
    lo = (rest - mid.astype(F32)).astype(BF16)
    tri = tri_ref[...]
    cum_s[...] = (jnp.dot(tri, hi, preferred_element_type=F32) + jnp.dot(tri, mid, preferred_element_type=F32)
                  + jnp.dot(tri, lo, preferred_element_type=F32))
    scale = GLA_DK ** -0.5
    contract_last = (((1,), (1,)), ((), ()))
    contract_first = (((0,), (0,)), ((), ()))
    heads = [(slice(h * GLA_DK, (h + 1) * GLA_DK), slice(h * GLA_DV, (h + 1) * GLA_DV)) for h in range(GLA_HEADS)]

    def finish(o, r0, rows, vs):
        o = o * lax.rsqrt(jnp.mean(o * o, axis=-1, keepdims=True) + NORM_EPS)
        rr = r_ref[pl.ds(r0, rows), vs]
        y_ref[pl.ds(r0, rows), vs] = (o * (gain_ref[:, vs] * rr * _sigmoid(rr))).astype(y_ref.dtype)

    def advance_state(h, kh, ch, vh, rows):
        last = ch[rows - 1:rows, :]
        khat = kh * jnp.exp(last - ch)
        st_ref[h] = st_ref[h] * jnp.exp(last) + lax.dot_general(
            vh.astype(BF16), khat.astype(BF16), contract_first, preferred_element_type=F32)

    def fast(r0):
        t_i = lax.broadcasted_iota(jnp.int32, (chunk, chunk), 0)
        s_i = lax.broadcasted_iota(jnp.int32, (chunk, chunk), 1)
        for h, (ks, vs) in enumerate(heads):
            qh = q_ref[pl.ds(r0, chunk), ks] * scale
            kh = k_ref[pl.ds(r0, chunk), ks]
            vh = v_ref[pl.ds(r0, chunk), vs]
            ch = cum_s[pl.ds(r0, chunk), ks]
            qd = (qh * jnp.exp(ch)).astype(BF16)
            kd = (kh * jnp.exp(-ch)).astype(BF16)
            sc = lax.dot_general(qd, kd, contract_last, preferred_element_type=F32)
            sc = jnp.where(s_i <= t_i, sc, 0.0).astype(BF16)
            o = jnp.dot(sc, vh.astype(BF16), preferred_element_type=F32) + lax.dot_general(
                qd, st_ref[h].astype(BF16), contract_last, preferred_element_type=F32)
            advance_state(h, kh, ch, vh, chunk)
            finish(o, r0, chunk, vs)

    def exact_block(r0, first):
        base = jnp.where(first, 0.0, cum_s[pl.ds(jnp.maximum(r0 - 1, 0), 1), :])
        t_i = lax.broadcasted_iota(jnp.int32, (sub, sub, 1), 0)
        s_i = lax.broadcasted_iota(jnp.int32, (sub, sub, 1), 1)
        for h, (ks, vs) in enumerate(heads):
            qh = q_ref[pl.ds(r0, sub), ks] * scale
            kh = k_ref[pl.ds(r0, sub), ks]
            vh = v_ref[pl.ds(r0, sub), vs]
            ch = cum_s[pl.ds(r0, sub), ks] - base[:, ks]
            diff = ch[:, None, :] - ch[None, :, :]
            dec = jnp.where(s_i <= t_i, jnp.exp(jnp.minimum(diff, 0.0)), 0.0)
            sc = jnp.sum(qh[:, None, :] * kh[None, :, :] * dec, axis=-1, keepdims=True)
            o = jnp.sum(sc * vh[None, :, :], axis=1) + lax.dot_general(
                (qh * jnp.exp(ch)).astype(BF16), st_ref[h].astype(BF16), contract_last, preferred_element_type=F32)
            advance_state(h, kh, ch, vh, sub)
            finish(o, r0, sub, vs)

    def per_chunk(c, carry):
        r0 = pl.multiple_of(c * chunk, sub)
        mild = jnp.min(cum_s[pl.ds(r0 + chunk - 1, 1), :]) >= -GLA_FAST_LIMIT

        @pl.when(mild)
        def _():
            fast(r0)

        @pl.when(jnp.logical_not(mild))
        def _():
            def block(i, carry2):
                exact_block(pl.multiple_of(r0 + i * sub, sub), i == 0)
                return carry2

            lax.fori_loop(0, chunk // sub, block, 0)

        return carry

    lax.fori_loop(0, tt // chunk, per_chunk, 0)


def _gla(q, k, v, r, gd, bsz, t, tt, g_up, g_b, head_gain):
    nt = t // tt
    kw, vw = GLA_KEY_WIDTH, GLA_VAL_WIDTH
    chunk = _pick_tile(tt, 144, GLA_SUB)
    gup = jnp.zeros((LANES, kw), F32).at[:GLA_LORA].set(g_up)
    i = jnp.arange(tt)
    tri = ((i[None, :] <= i[:, None]) & (i[None, :] // chunk == i[:, None] // chunk)).astype(BF16)
    spec = lambda w: pl.BlockSpec((tt, w), lambda b, j: (b * nt + j, 0))
    return pl.pallas_call(
        functools.partial(_gla_kernel, chunk=chunk),
        grid=(bsz, nt),
        in_specs=[spec(kw), spec(kw), spec(vw), spec(vw), spec(LANES), _const_spec((LANES, kw)),
                  _const_spec((1, kw)), _const_spec((1, vw)), _const_spec((tt, tt))],
        out_specs=spec(vw),
        out_shape=jax.ShapeDtypeStruct((bsz * t, vw), BF16),
        scratch_shapes=[pltpu.VMEM((GLA_HEADS, GLA_DV, GLA_DK), F32), pltpu.VMEM((tt, kw), F32)],
        compiler_params=_params("parallel", "arbitrary"),
        name="gla",
    )(q, k, v, r, gd, gup.astype(BF16), g_b.reshape(1, -1), head_gain.reshape(1, -1), tri)


ROUTE_E1, ROUTE_E2, ROUTE_RANK1, ROUTE_RANK2, ROUTE_W1, ROUTE_W2 = range(6)


def _rms(x, gain):
    return x * lax.rsqrt(jnp.mean(x * x, axis=-1, keepdims=True) + NORM_EPS) * gain


def _moe_route_kernel(h_ref, gain_ref, wr_ref, br_ref, tri_ref, info_ref, cnt_ref, carry_s):
    @pl.when(pl.program_id(0) == 0)
    def _():
        carry_s[...] = jnp.zeros_like(carry_s)

    lane = lax.broadcasted_iota(jnp.int32, (1, LANES), 1)
    xn = _rms(h_ref[...], gain_ref[...])
    logits = jnp.dot(xn, wr_ref[...], preferred_element_type=F32, precision=lax.Precision.HIGHEST) + br_ref[...]
    neg = -jnp.inf
    far = 4 * LANES
    first = lambda hit: jnp.min(jnp.where(hit, lane, far), axis=-1, keepdims=True)
    is_group = lane < N_GROUPS
    gl = jnp.where(is_group, logits, neg)
    gmax = jnp.max(gl, axis=-1, keepdims=True)
    g_idx = first(gl == gmax)
    g_prob = 1.0 / jnp.sum(jnp.where(is_group, jnp.exp(logits - gmax), 0.0), axis=-1, keepdims=True)
    in_group = (lane >= N_GROUPS) & (lane < N_GROUPS + N_EXPERTS) & (((lane - N_GROUPS) >> 3) == g_idx)
    el = jnp.where(in_group, logits, neg)
    v1 = jnp.max(el, axis=-1, keepdims=True)
    i1 = first(el == v1)
    el2 = jnp.where(lane == i1, neg, el)
    v2 = jnp.max(el2, axis=-1, keepdims=True)
    i2 = first(el2 == v2)
    e21 = jnp.exp(v2 - v1)
    w1 = g_prob / (1.0 + e21)
    hit1 = lane == i1
    hit2 = lane == i2
    m = jnp.where(hit1 | hit2, 1.0, 0.0)
    before = jnp.dot(tri_ref[...], m.astype(BF16), preferred_element_type=F32) + carry_s[...]
    rank1 = jnp.sum(jnp.where(hit1, before, 0.0), axis=-1, keepdims=True)
    rank2 = jnp.sum(jnp.where(hit2, before, 0.0), axis=-1, keepdims=True)
    carry_s[...] += jnp.sum(m, axis=0, keepdims=True)
    cnt_ref[...] = carry_s[...]
    fields = ((i1 - N_GROUPS).astype(F32), (i2 - N_GROUPS).astype(F32), rank1, rank2, w1, w1 * e21)
    info = jnp.zeros(info_ref.shape, F32)
    for pos, val in enumerate(fields):
        info = jnp.where(lane == pos, val, info)
    info_ref[...] = info


def _row_copy(src, src_row, dst, dst_row, sem):
    return pltpu.make_async_copy(src.at[pl.ds(src_row, 1)], dst.at[pl.ds(dst_row, 1)], sem)


def _start_then_wait(copies, n):
    def start(j, c):
        for cp in copies(j):
            cp.start()
        return c

    def wait(j, c):
        for cp in copies(j):
            cp.wait()
        return c

    lax.fori_loop(0, n, start, 0, unroll=DMA_UNROLL)
    lax.fori_loop(0, n, wait, 0, unroll=DMA_UNROLL)


def _moe_pos_kernel(info_ref, offs_ref, o_ref):
    info = info_ref[...]
    lane = lax.broadcasted_iota(jnp.int32, (1, LANES), 1)
    out = info
    for e_lane, rank_lane in ((ROUTE_E1, ROUTE_RANK1), (ROUTE_E2, ROUTE_RANK2)):
        e = info[:, e_lane:e_lane + 1].astype(jnp.int32)
        off = jnp.sum(jnp.where(lane == e, offs_ref[...], 0.0), axis=-1, keepdims=True)
        out = jnp.where(lane == rank_lane, info[:, rank_lane:rank_lane + 1] + off, out)
    o_ref[...] = out


def _moe_dispatch_kernel(pos1_ref, pos2_ref, h_ref, init_hbm, xs_hbm, sem):
    del init_hbm
    _start_then_wait(lambda j: (_row_copy(h_ref, j, xs_hbm, pos1_ref[0, 0, j], sem.at[0]),
                                _row_copy(h_ref, j, xs_hbm, pos2_ref[0, 0, j], sem.at[1])), h_ref.shape[0])


def _moe_ffn_kernel(te_ref, xs_ref, gain_ref, wg_ref, wu_ref, wd_ref, ys_ref):
    del te_ref
    xn = _rms(xs_ref[...], gain_ref[...]).astype(BF16)
    gate = jnp.dot(xn, wg_ref[0], preferred_element_type=F32)
    up = jnp.dot(xn, wu_ref[0], preferred_element_type=F32)
    hid = gate * _sigmoid(gate) * up
    ys_ref[...] = jnp.dot(hid.astype(BF16), wd_ref[0], preferred_element_type=F32)


def _moe_combine_kernel(pos1_ref, pos2_ref, h_ref, info_ref, ys_hbm, o_ref, buf1, buf2, sem):
    _start_then_wait(lambda j: (_row_copy(ys_hbm, pos1_ref[0, 0, j], buf1, j, sem.at[0]),
                                _row_copy(ys_hbm, pos2_ref[0, 0, j], buf2, j, sem.at[1])), h_ref.shape[0])
    info = info_ref[...]
    w1 = info[:, ROUTE_W1:ROUTE_W1 + 1]
    w2 = info[:, ROUTE_W2:ROUTE_W2 + 1]
    o_ref[...] = h_ref[...] + w1 * buf1[...] + w2 * buf2[...]


def _moe(h, gain, w_group, b_group, w_router, b_router, w_gate, w_up, w_down, tm, tg):
    n, d = h.shape
    nt = n // tm
    wr = jnp.zeros((d, LANES), F32).at[:, :N_GROUPS].set(w_group).at[:, N_GROUPS:N_GROUPS + N_EXPERTS].set(w_router)
    br = jnp.zeros((1, LANES), F32).at[0, :N_GROUPS].set(b_group).at[0, N_GROUPS:N_GROUPS + N_EXPERTS].set(b_router)
    i = jnp.arange(tm)
    tri = (i[None, :] < i[:, None]).astype(BF16)
    info, cnt = pl.pallas_call(
        _moe_route_kernel,
        grid=(nt,),
        in_specs=[pl.BlockSpec((tm, d), lambda i: (i, 0)), _const_spec((1, d)), _const_spec((d, LANES)),
                  _const_spec((1, LANES)), _const_spec((tm, tm))],
        out_specs=[pl.BlockSpec((tm, LANES), lambda i: (i, 0)), _const_spec((1, LANES))],
        out_shape=[jax.ShapeDtypeStruct((n, LANES), F32), jax.ShapeDtypeStruct((1, LANES), F32)],
        scratch_shapes=[pltpu.VMEM((1, LANES), F32)],
        compiler_params=_params("arbitrary"),
        name="moe_route",
    )(h, gain.reshape(1, d), wr, br, tri)

    counts = cnt[:, N_GROUPS:N_GROUPS + N_EXPERTS].astype(jnp.int32)
    padded = (counts + tg - 1) // tg * tg
    ends = jnp.cumsum(padded, axis=1)
    offs = jnp.zeros((1, LANES), F32).at[:, :N_EXPERTS].set((ends - padded).astype(F32))
    n_rows = (2 * n + tg - 1) // tg * tg + N_EXPERTS * tg
    n_tiles = n_rows // tg
    tile_expert = jnp.minimum(jnp.sum(jnp.arange(n_tiles)[:, None] * tg >= ends, axis=1),
                              N_EXPERTS - 1).astype(jnp.int32)
    info = pl.pallas_call(
        _moe_pos_kernel,
        grid=(nt,),
        in_specs=[pl.BlockSpec((tm, LANES), lambda i: (i, 0)), _const_spec((1, LANES))],
        out_specs=pl.BlockSpec((tm, LANES), lambda i: (i, 0)),
        out_shape=jax.ShapeDtypeStruct((n, LANES), F32),
        compiler_params=_params("parallel"),
        name="moe_pos",
    )(info, offs)
    pos1 = info[:, ROUTE_RANK1].astype(jnp.int32).reshape(nt, 1, tm)
    pos2 = info[:, ROUTE_RANK2].astype(jnp.int32).reshape(nt, 1, tm)

    pos_spec = pl.BlockSpec((1, 1, tm), lambda i: (i, 0, 0), memory_space=pltpu.SMEM)
    any_spec = pl.BlockSpec(memory_space=pl.ANY)
    xs = pl.pallas_call(
        _moe_dispatch_kernel,
        grid=(nt,),
        in_specs=[pos_spec, pos_spec, pl.BlockSpec((tm, d), lambda i: (i, 0)), any_spec],
        out_specs=any_spec,
        out_shape=jax.ShapeDtypeStruct((n_rows, d), F32),
        scratch_shapes=[pltpu.SemaphoreType.DMA((2,))],
        input_output_aliases={3: 0},
        compiler_params=_params("arbitrary"),
        name="moe_dispatch",
    )(pos1, pos2, h, jnp.zeros((n_rows, d), F32))

    ys = pl.pallas_call(
        _moe_ffn_kernel,
        grid_spec=pltpu.PrefetchScalarGridSpec(
            num_scalar_prefetch=1,
            grid=(n_tiles,),
            in_specs=[pl.BlockSpec((tg, d), lambda i, te: (i, 0)),
                      pl.BlockSpec((1, d), lambda i, te: (0, 0)),
                      pl.BlockSpec((1, d, EXPERT_FF), lambda i, te: (te[i], 0, 0)),
                      pl.BlockSpec((1, d, EXPERT_FF), lambda i, te: (te[i], 0, 0)),
                      pl.BlockSpec((1, EXPERT_FF, d), lambda i, te: (te[i], 0, 0))],
            out_specs=pl.BlockSpec((tg, d), lambda i, te: (i, 0))),
        out_shape=jax.ShapeDtypeStruct((n_rows, d), F32),
        compiler_params=_params("arbitrary"),
        name="moe_ffn",
    )(tile_expert, xs, gain.reshape(1, d), w_gate.astype(BF16), w_up.astype(BF16), w_down.astype(BF16))

    return pl.pallas_call(
        _moe_combine_kernel,
        grid=(nt,),
        in_specs=[pos_spec, pos_spec, pl.BlockSpec((tm, d), lambda i: (i, 0)),
                  pl.BlockSpec((tm, LANES), lambda i: (i, 0)), any_spec],
        out_specs=pl.BlockSpec((tm, d), lambda i: (i, 0)),
        out_shape=jax.ShapeDtypeStruct((n, d), F32),
        scratch_shapes=[pltpu.VMEM((tm, d), F32), pltpu.VMEM((tm, d), F32), pltpu.SemaphoreType.DMA((2,))],
        compiler_params=_params("arbitrary"),
        name="moe_combine",
    )(pos1, pos2, h, info, ys)


def _final_norm_kernel(x_ref, g_ref, o_ref):
    x = x_ref[...]
    o_ref[...] = x * lax.rsqrt(jnp.mean(x * x, axis=-1, keepdims=True) + NORM_EPS) * g_ref[...]


def _final_norm(h, gain, tm):
    n, d = h.shape
    return pl.pallas_call(
        _final_norm_kernel,
        grid=(n // tm,),
        in_specs=[pl.BlockSpec((tm, d), lambda i: (i, 0)), _const_spec((1, d))],
        out_specs=pl.BlockSpec((tm, d), lambda i: (i, 0)),
        out_shape=jax.ShapeDtypeStruct((n, d), F32),
        compiler_params=_params("parallel"),
        name="final_norm",
    )(h, gain.reshape(1, d))


def _even_mixer(h, bsz, t, tm, tt, ts, norm, w_in, mu, decay0, decay_up, iclr0, iclr_up, gate_up, k_k, k_a, r_k,
                ln_g, ln_b, conv_w, conv_b, lru_wa, lru_ba, lru_wx, lru_bx, lru_lambda, w_out):
    p, xb, gb = _norm_proj(h, norm, w_in, (RWKV_IN, LRU_WIDTH, LRU_WIDTH), tm)
    r, w, k, v, kk, b, g, bonus = _rwkv_prep(p, bsz, t, tt, mu, decay0, decay_up, iclr0, iclr_up, gate_up, k_k,
                                             k_a, r_k)
    o = _rwkv_chunked(r, w, k, v, kk, b, bsz, t, tt)
    y_a = _rwkv_post(o, bonus, g, ln_g, ln_b, bsz, t, tt)
    y_b = _lru(xb, gb, bsz, t, tt, conv_w, conv_b, lru_wa, lru_ba, lru_wx, lru_bx, lru_lambda)
    return _out_proj(h, (y_a, y_b), (w_out[:RWKV_WIDTH], w_out[RWKV_WIDTH:]), tm)


def _odd_mixer(h, bsz, t, tm, tt, norm, w_in, g_up, g_b, head_gain, w_out):
    kw, vw = GLA_KEY_WIDTH, GLA_VAL_WIDTH
    d = w_in.shape[0]
    w_packed = jnp.concatenate(
        [w_in[:, :2 * kw + vw], w_in[:, 2 * kw + vw + GLA_LORA:], w_in[:, 2 * kw + vw:2 * kw + vw + GLA_LORA],
         jnp.zeros((d, LANES - GLA_LORA), w_in.dtype)], axis=1)
    q, k, v, r, gd = _norm_proj(h, norm, w_packed, (kw, kw, vw, vw, LANES), tm)
    y = _gla(q, k, v, r, gd, bsz, t, tt, g_up, g_b, head_gain)
    return _out_proj(h, (y,), (w_out,), tm)


def kernel(x, meta, e_norm, e_w_in, e_mu, e_decay0, e_decay_up, e_iclr0, e_iclr_up, e_gate_up, e_k_k, e_k_a, e_r_k, e_ln_g, e_ln_b, e_conv_w, e_conv_b, e_lru_wa, e_lru_ba, e_lru_wx, e_lru_bx, e_lru_lambda, e_w_out, o_norm, o_w_in, o_g_up, o_g_b, o_head_gain, o_w_out, m_norm, m_w_group, m_b_group, m_w_router, m_b_router, m_w_gate, m_w_up, m_w_down, final_norm):
    bsz, seq, d = x.shape
    t = seq + N_META
    n = bsz * t
    tm = _pick_tile(n, 384)
    tt = _pick_tile(t, 432)
    ts = _pick_tile(t, 48)
    depth = m_norm.shape[0]
    h = jnp.concatenate([jnp.broadcast_to(meta[None].astype(x.dtype), (bsz, N_META, d)), x], axis=1).reshape(n, d)
    for layer in range(depth):
        i = layer // 2
        if layer % 2 == 0:
            h = _even_mixer(h, bsz, t, tm, tt, ts, e_norm[i], e_w_in[i], e_mu[i], e_decay0[i], e_decay_up[i],
                            e_iclr0[i], e_iclr_up[i], e_gate_up[i], e_k_k[i], e_k_a[i], e_r_k[i], e_ln_g[i],
                            e_ln_b[i], e_conv_w[i], e_conv_b[i], e_lru_wa[i], e_lru_ba[i], e_lru_wx[i],
                            e_lru_bx[i], e_lru_lambda[i], e_w_out[i])
        else:
            h = _odd_mixer(h, bsz, t, tm, tt, o_norm[i], o_w_in[i], o_g_up[i], o_g_b[i], o_head_gain[i], o_w_out[i])
        h = _moe(h, m_norm[layer], m_w_group[layer], m_b_group[layer], m_w_router[layer], m_b_router[layer],
                 m_w_gate[layer], m_w_up[layer], m_w_down[layer], tm, MOE_TILE)
    out = _final_norm(h, final_norm, tm)
    return out.reshape(bsz, t, d)[:, N_META:]
```

```python
import functools

import jax
import jax.numpy as jnp
from jax import lax
from jax.experimental import pallas as pl
from jax.experimental.pallas import tpu as pltpu

F32 = jnp.float32
BF16 = jnp.bfloat16

N_META = 16
NORM_EPS = 1e-6
RWKV_HEADS = 8
HEAD_DIM = 64
RWKV_WIDTH = 512
LORA_W = 64
LORA_A = 64
LORA_G = 128
RWKV_IN = 3 * RWKV_WIDTH + LORA_W + LORA_A + LORA_G
GN_EPS = 64e-5
LRU_WIDTH = 512
LRU_C = 8.0
GLA_HEADS = 4
GLA_DK = 128
GLA_DV = 256
GLA_KEY_WIDTH = GLA_HEADS * GLA_DK
GLA_VAL_WIDTH = GLA_HEADS * GLA_DV
GLA_LORA = 16
GLA_TAU = 16.0
GLA_SUB = 16
N_GROUPS = 4
EXPERTS_PER_GROUP = 8
N_EXPERTS = N_GROUPS * EXPERTS_PER_GROUP
EXPERT_FF = 256
MOE_TILE = 256
DMA_UNROLL = 8

LANES = 128
SUBLANES = 8
VMEM_LIMIT = 48 * 1024 * 1024


def _pick_tile(n, target, mult=16):
    best = None
    for t in range(mult, min(n, target) + 1, mult):
        if n % t == 0:
            best = t
    assert best is not None, (n, target, mult)
    return best


def _params(*sem):
    return pltpu.CompilerParams(dimension_semantics=sem, vmem_limit_bytes=VMEM_LIMIT)


def _softplus(x):
    return jnp.maximum(x, 0.0) + jnp.log(1.0 + jnp.exp(-jnp.abs(x)))


def _sigmoid(x):
    return 1.0 / (1.0 + jnp.exp(-x))


def _block_ones(width, block, dtype):
    r = lax.broadcasted_iota(jnp.int32, (width, width), 0) // block
    c = lax.broadcasted_iota(jnp.int32, (width, width), 1) // block
    return (r == c).astype(dtype)


def _const_spec(shape):
    zeros = (0,) * len(shape)
    return pl.BlockSpec(shape, lambda *_: zeros)


def _norm_proj_kernel(x_ref, g_ref, w_ref, *o_refs):
    x = x_ref[...]
    y = x * lax.rsqrt(jnp.mean(x * x, axis=-1, keepdims=True) + NORM_EPS) * g_ref[...]
    y = y.astype(BF16)
    off = 0
    for o_ref in o_refs:
        width = o_ref.shape[-1]
        o_ref[...] = jnp.dot(y, w_ref[:, off:off + width], preferred_element_type=F32).astype(o_ref.dtype)
        off += width


def _norm_proj(x, gain, w, splits, tm):
    n, d = x.shape
    f = w.shape[1]
    assert sum(splits) == f and all(s % LANES == 0 for s in splits)
    return pl.pallas_call(
        _norm_proj_kernel,
        grid=(n // tm,),
        in_specs=[pl.BlockSpec((tm, d), lambda i: (i, 0)), _const_spec((1, d)), _const_spec((d, f))],
        out_specs=[pl.BlockSpec((tm, s), lambda i: (i, 0)) for s in splits],
        out_shape=[jax.ShapeDtypeStruct((n, s), F32) for s in splits],
        compiler_params=_params("parallel"),
        name="norm_proj",
    )(x, gain.reshape(1, d), w.astype(BF16))


def _out_proj_kernel(*refs):
    h_ref, o_ref = refs[0], refs[-1]
    pairs = refs[1:-1]
    acc = h_ref[...]
    for y_ref, w_ref in zip(pairs[0::2], pairs[1::2]):
        acc = acc + jnp.dot(y_ref[...], w_ref[...], preferred_element_type=F32)
    o_ref[...] = acc


def _out_proj(h, ys, ws, tm):
    n, d = h.shape
    in_specs = [pl.BlockSpec((tm, d), lambda i: (i, 0))]
    args = [h]
    for y, w in zip(ys, ws):
        in_specs += [pl.BlockSpec((tm, y.shape[1]), lambda i: (i, 0)), _const_spec(w.shape)]
        args += [y, w.astype(BF16)]
    return pl.pallas_call(
        _out_proj_kernel,
        grid=(n // tm,),
        in_specs=in_specs,
        out_specs=pl.BlockSpec((tm, d), lambda i: (i, 0)),
        out_shape=jax.ShapeDtypeStruct((n, d), F32),
        compiler_params=_params("parallel"),
        name="out_proj",
    )(*args)


def _rwkv_prep_kernel(p_ref, mu_ref, lora_ref, dec0_ref, iclr0_ref, gup_ref, kk_ref, ka_ref, rk_ref, ones_ref,
                      r_o, w_o, k_o, v_o, kkn_o, b_o, g_o, bonus_o, carry_ref):
    tt = p_ref.shape[0]
    width = RWKV_WIDTH

    @pl.when(pl.program_id(1) == 0)
    def _():
        carry_ref[...] = jnp.zeros_like(carry_ref)

    p = p_ref[...]
    row = lax.broadcasted_iota(jnp.int32, (tt, 1), 0)
    prev = jnp.where(row == 0, carry_ref[...], pltpu.roll(p, 1, axis=0))
    carry_ref[...] = p[tt - 1:tt, :]
    xs = p + mu_ref[...] * (prev - p)
    r = xs[:, 0:width]
    k = xs[:, width:2 * width]
    v = xs[:, 2 * width:3 * width]
    wa = xs[:, 3 * width:3 * width + LANES]
    gd = xs[:, 3 * width + LANES:]
    lane = lax.broadcasted_iota(jnp.int32, (1, LANES), 1)
    wa = jnp.where(lane < LORA_W, jnp.tanh(wa), wa)
    lora = jnp.dot(wa.astype(BF16), lora_ref[...], preferred_element_type=F32)
    wlog = -_softplus(-(dec0_ref[...] + lora[:, :width])) - 0.5
    a = _sigmoid(iclr0_ref[...] + lora[:, width:])
    g = jnp.dot(_sigmoid(gd).astype(BF16), gup_ref[...], preferred_element_type=F32)
    ones = ones_ref[...]
    kraw = k * kk_ref[...]
    ss = jnp.dot((kraw * kraw).astype(BF16), ones, preferred_element_type=F32)
    kkn = kraw / jnp.maximum(jnp.sqrt(ss), 1e-12)
    k_mod = k * (1.0 + (a - 1.0) * ka_ref[...])
    bonus = jnp.dot((r * k_mod * rk_ref[...]).astype(BF16), ones, preferred_element_type=F32) * v
    r_o[...] = r
    w_o[...] = -jnp.exp(wlog)
    k_o[...] = k_mod
    v_o[...] = v
    kkn_o[...] = kkn
    b_o[...] = kkn * a
    g_o[...] = g
    bonus_o[...] = bonus


def _rwkv_prep(p, bsz, t, tt, mu, decay0, decay_up, iclr0, iclr_up, gate_up, k_k, k_a, r_k):
    nt = t // tt
    width = RWKV_WIDTH
    lora_w = jnp.zeros((LANES, 2 * width), F32)
    lora_w = lora_w.at[:LORA_W, :width].set(decay_up).at[LORA_W:, width:].set(iclr_up)
    row = lambda x: x.reshape(1, -1)
    tok_spec = pl.BlockSpec((tt, width), lambda b, i: (b * nt + i, 0))
    outs = pl.pallas_call(
        _rwkv_prep_kernel,
        grid=(bsz, nt),
        in_specs=[pl.BlockSpec((tt, RWKV_IN), lambda b, i: (b * nt + i, 0)),
                  _const_spec((1, RWKV_IN)), _const_spec((LANES, 2 * width)), _const_spec((1, width)),
                  _const_spec((1, width)), _const_spec((LORA_G, width)), _const_spec((1, width)),
                  _const_spec((1, width)), _const_spec((1, width)), _const_spec((width, width))],
        out_specs=[tok_spec] * 8,
        out_shape=[jax.ShapeDtypeStruct((bsz * t, width), F32)] * 8,
        scratch_shapes=[pltpu.VMEM((1, RWKV_IN), F32)],
        compiler_params=_params("parallel", "arbitrary"),
        name="rwkv_prep",
    )(p, row(mu), lora_w.astype(BF16), row(decay0), row(iclr0), gate_up.astype(BF16), row(k_k), row(k_a),
      row(r_k), _block_ones(width, HEAD_DIM, BF16))
    return outs


RWKV_CHUNK = 48


def _rwkv_chunk_kernel(r_ref, lw_ref, k_ref, v_ref, kk_ref, b_ref, tri_ref, o_ref, st_ref, *, chunk):
    streams, tt, _ = r_ref.shape
    ln = chunk
    pairs = streams * (RWKV_WIDTH // LANES)

    @pl.when(pl.program_id(1) == 0)
    def _():
        st_ref[...] = jnp.zeros_like(st_ref)

    tri = tri_ref[...]
    head0 = lax.broadcasted_iota(jnp.int32, (1, LANES), 1) < HEAD_DIM
    t_i = lax.broadcasted_iota(jnp.int32, (ln, 2 * ln), 0)
    col = lax.broadcasted_iota(jnp.int32, (ln, 2 * ln), 1)
    s_i = jnp.where(col < ln, col, col - ln)
    strict = s_i < t_i
    incl = s_i <= t_i
    same_head = ((lax.broadcasted_iota(jnp.int32, (LANES, LANES), 0) < HEAD_DIM)
                 == (lax.broadcasted_iota(jnp.int32, (LANES, LANES), 1) < HEAD_DIM))
    contract_last = (((1,), (1,)), ((), ()))
    contract_first = (((0,), (0,)), ((), ()))

    def by_head(z):
        return jnp.concatenate([jnp.where(head0, z, 0.0), jnp.where(head0, 0.0, z)], axis=0).astype(BF16)

    def per_chunk(ci, carry):
        r0 = pl.multiple_of(ci * ln, SUBLANES)
        rows = pl.ds(r0, ln)
        e_c, e_prev, e_inv, e_rest = [], [], [], []
        for i in range(streams):
            lw = lw_ref[i, rows, :]
            hi = lw.astype(BF16)
            rest = lw - hi.astype(F32)
            mid = rest.astype(BF16)
            lo = (rest - mid.astype(F32)).astype(BF16)
            c = (jnp.dot(tri, hi, preferred_element_type=F32) + jnp.dot(tri, mid, preferred_element_type=F32)
                 + jnp.dot(tri, lo, preferred_element_type=F32))
            e_c.append(jnp.exp(c))
            e_prev.append(jnp.exp(c - lw))
            e_inv.append(jnp.exp(-c))
            e_rest.append(jnp.exp(c[ln - 1:ln, :] - c))
        prs = range(pairs)
        per = RWKV_WIDTH // LANES
        lanes = [(p // per, slice(p % per * LANES, (p % per + 1) * LANES)) for p in prs]
        kt, rt, sb, sk, npair, st, stb, vblk, u = ([None] * pairs for _ in range(9))
        for p, (i, ps) in enumerate(lanes):
            kt[p] = kk_ref[i, rows, ps] * e_prev[i][:, ps]
            rt[p] = r_ref[i, rows, ps] * e_c[i][:, ps]
            x = jnp.concatenate([kt[p], rt[p]], axis=0).astype(BF16)
            sb[p] = lax.dot_general(x, by_head(b_ref[i, rows, ps] * e_inv[i][:, ps]), contract_last,
                                    preferred_element_type=F32)
            sk[p] = lax.dot_general(x, by_head(k_ref[i, rows, ps] * e_inv[i][:, ps]), contract_last,
                                    preferred_element_type=F32)
            npair[p] = jnp.where(strict, sb[p][:ln], 0.0)
            st[p] = st_ref[p]
            stb[p] = st[p].astype(BF16)
            vblk[p] = by_head(v_ref[i, rows, ps])
            u[p] = (lax.dot_general(kt[p].astype(BF16), stb[p], contract_last, preferred_element_type=F32)
                    + jnp.dot(jnp.where(strict, sk[p][:ln], 0.0).astype(BF16), vblk[p], preferred_element_type=F32))
        solved = [[] for _ in prs]
        for j in range(ln // SUBLANES):
            lo8 = j * SUBLANES
            blk = [u[p][:SUBLANES] for p in prs]
            for s in range(SUBLANES - 1):
                for p in prs:
                    nj = npair[p][lo8:lo8 + SUBLANES]
                    coef = jnp.where(head0, nj[:, lo8 + s:lo8 + s + 1], nj[:, ln + lo8 + s:ln + lo8 + s + 1])
                    blk[p] = blk[p] - coef * blk[p][s:s + 1]
            for p in prs:
                solved[p].append(blk[p])
                if lo8 + SUBLANES < ln:
                    zero = jnp.zeros((SUBLANES, LANES), F32)
                    pieces = [zero] * (2 * ln // SUBLANES)
                    pieces[j] = jnp.where(head0, blk[p], 0.0)
                    pieces[ln // SUBLANES + j] = jnp.where(head0, 0.0, blk[p])
                    u[p] = u[p][SUBLANES:] - jnp.dot(npair[p][lo8 + SUBLANES:].astype(BF16),
                                                     jnp.concatenate(pieces, axis=0).astype(BF16),
                                                     preferred_element_type=F32)
        for p, (i, ps) in enumerate(lanes):
            up = jnp.concatenate(solved[p], axis=0)
            o = (lax.dot_general(rt[p].astype(BF16), stb[p], contract_last, preferred_element_type=F32)
                 + jnp.dot(jnp.where(incl, sk[p][ln:], 0.0).astype(BF16), vblk[p], preferred_element_type=F32)
                 - jnp.dot(jnp.where(incl, sb[p][ln:], 0.0).astype(BF16), by_head(up), preferred_element_type=F32))
            o_ref[i, rows, ps] = o
            grow = (lax.dot_general(v_ref[i, rows, ps].astype(BF16), (k_ref[i, rows, ps] * e_rest[i][:, ps]).astype(BF16),
                                    contract_first, preferred_element_type=F32)
                    - lax.dot_general(up.astype(BF16), (b_ref[i, rows, ps] * e_rest[i][:, ps]).astype(BF16), contract_first,
                                      preferred_element_type=F32))
            st_ref[p] = st[p] * e_c[i][ln - 1:ln, ps] + jnp.where(same_head, grow, 0.0)
        return carry

    lax.fori_loop(0, tt // ln, per_chunk, 0)


def _rwkv_chunked(r, lw, k, v, kk, b, bsz, t, tt):
    width = RWKV_WIDTH
    nt = t // tt
    chunk = RWKV_CHUNK
    assert tt % chunk == 0
    streams = 2 if bsz % 2 == 0 else 1
    nb = bsz // streams
    i = jnp.arange(chunk)
    tri = (i[None, :] <= i[:, None]).astype(BF16)
    spec = pl.BlockSpec((streams, tt, width), lambda bb, j: (0, bb * nt + j, 0))
    out = pl.pallas_call(
        functools.partial(_rwkv_chunk_kernel, chunk=chunk),
        grid=(nb, nt),
        in_specs=[spec] * 6 + [_const_spec((chunk, chunk))],
        out_specs=spec,
        out_shape=jax.ShapeDtypeStruct((streams, nb * t, width), F32),
        scratch_shapes=[pltpu.VMEM((streams * (width // LANES), LANES, LANES), F32)],
        compiler_params=_params("parallel", "arbitrary"),
        name="rwkv_chunked",
    )(*(x.reshape(streams, nb * t, width) for x in (r, lw, k, v, kk, b)), tri)
    return out.reshape(bsz * t, width)


def _rwkv_post_kernel(o_ref, bonus_ref, g_ref, lng_ref, lnb_ref, ones_ref, y_ref):
    ones = ones_ref[...]

    def seg_mean(x):
        hi = x.astype(BF16)
        lo = (x - hi.astype(F32)).astype(BF16)
        return (jnp.dot(hi, ones, preferred_element_type=F32)
                + jnp.dot(lo, ones, preferred_element_type=F32)) * (1.0 / HEAD_DIM)

    o = o_ref[...]
    d = o - seg_mean(o)
    var = seg_mean(d * d)
    y = d * lax.rsqrt(var + GN_EPS) * lng_ref[...] + lnb_ref[...]
    y_ref[...] = ((y + bonus_ref[...]) * g_ref[...]).astype(y_ref.dtype)


def _rwkv_post(o, bonus, g, ln_g, ln_b, bsz, t, tt):
    nt = t // tt
    width = RWKV_WIDTH
    tok_spec = pl.BlockSpec((tt, width), lambda b, i: (b * nt + i, 0))
    return pl.pallas_call(
        _rwkv_post_kernel,
        grid=(bsz, nt),
        in_specs=[tok_spec] * 3 + [_const_spec((1, width))] * 2 + [_const_spec((width, width))],
        out_specs=tok_spec,
        out_shape=jax.ShapeDtypeStruct((bsz * t, width), BF16),
        compiler_params=_params("parallel", "parallel"),
        name="rwkv_post",
    )(o, bonus, g, ln_g.reshape(1, -1), ln_b.reshape(1, -1), _block_ones(width, HEAD_DIM, BF16))


def _lru_kernel(xb_ref, gb_ref, cw_ref, cb_ref, wa_ref, ba_ref, wx_ref, bx_ref, lam_ref, y_ref,
                hist_ref, hlast_ref, a_s, u_s):
    tt = xb_ref.shape[0]
    sub = SUBLANES

    @pl.when(pl.program_id(1) == 0)
    def _():
        hist_ref[...] = jnp.zeros_like(hist_ref)
        hlast_ref[...] = jnp.zeros_like(hlast_ref)

    xb = xb_ref[...]
    hist = hist_ref[...]
    hist_ref[...] = xb[tt - sub:, :]
    row8 = lax.broadcasted_iota(jnp.int32, (sub, 1), 0)
    cw = cw_ref[...]
    xc = xb * cw[3:4, :] + cb_ref[...]
    for s in (1, 2, 3):
        rolled = pltpu.roll(xb, s, axis=0)
        top = jnp.where(row8 < s, pltpu.roll(hist, s, axis=0), rolled[:sub, :])
        shifted = jnp.concatenate([top, rolled[sub:, :]], axis=0)
        xc = xc + shifted * cw[3 - s:4 - s, :]
    xcb = xc.astype(BF16)
    gate_r = _sigmoid(jnp.dot(xcb, wa_ref[...], preferred_element_type=F32) + ba_ref[...])
    gate_i = _sigmoid(jnp.dot(xcb, wx_ref[...], preferred_element_type=F32) + bx_ref[...])
    log_a = -LRU_C * gate_r * _softplus(-lam_ref[...])
    a_s[...] = jnp.exp(log_a)
    u_s[...] = jnp.sqrt(1.0 - jnp.exp(2.0 * log_a)) * gate_i * xc

    def group(j, h_prev):
        r0 = pl.multiple_of(j * sub, sub)
        a = a_s[pl.ds(r0, sub), :]
        u = u_s[pl.ds(r0, sub), :]
        for d in (1, 2, 4):
            keep = row8 >= d
            u = jnp.where(keep, a * pltpu.roll(u, d, axis=0) + u, u)
            a = jnp.where(keep, a * pltpu.roll(a, d, axis=0), a)
        h = u + a * h_prev
        u_s[pl.ds(r0, sub), :] = h
        return jnp.broadcast_to(h[sub - 1:sub, :], h.shape)

    h_last = lax.fori_loop(0, tt // sub, group, hlast_ref[...])
    hlast_ref[...] = h_last
    y_ref[...] = (u_s[...] * jax.nn.gelu(gb_ref[...])).astype(y_ref.dtype)


def _block_diag(w):
    nb, bi, bo = w.shape
    out = jnp.zeros((nb * bi, nb * bo), w.dtype)
    for i in range(nb):
        out = out.at[i * bi:(i + 1) * bi, i * bo:(i + 1) * bo].set(w[i])
    return out


def _lru(xb, gb, bsz, t, tt, conv_w, conv_b, wa, ba, wx, bx, lam):
    nt = t // tt
    width = LRU_WIDTH
    row = lambda x: x.reshape(1, -1)
    spec = pl.BlockSpec((tt, width), lambda b, i: (b * nt + i, 0))
    return pl.pallas_call(
        _lru_kernel,
        grid=(bsz, nt),
        in_specs=[spec, spec, _const_spec((4, width)), _const_spec((1, width)), _const_spec((width, width)),
                  _const_spec((1, width)), _const_spec((width, width)), _const_spec((1, width)),
                  _const_spec((1, width))],
        out_specs=spec,
        out_shape=jax.ShapeDtypeStruct((bsz * t, width), BF16),
        scratch_shapes=[pltpu.VMEM((SUBLANES, width), F32), pltpu.VMEM((SUBLANES, width), F32),
                        pltpu.VMEM((tt, width), F32), pltpu.VMEM((tt, width), F32)],
        compiler_params=_params("parallel", "arbitrary"),
        name="rglru",
    )(xb, gb, conv_w, row(conv_b), _block_diag(wa).astype(BF16), row(ba), _block_diag(wx).astype(BF16), row(bx),
      row(lam))


GLA_FAST_LIMIT = 60.0


def _gla_kernel(q_ref, k_ref, v_ref, r_ref, gd_ref, gup_ref, gb_ref, gain_ref, tri_ref, y_ref, st_ref, cum_s, *,
                chunk):
    tt = q_ref.shape[0]
    sub = GLA_SUB

    @pl.when(pl.program_id(1) == 0)
    def _():
        st_ref[...] = jnp.zeros_like(st_ref)

    logits = jnp.dot(gd_ref[...].astype(BF16), gup_ref[...], preferred_element_type=F32) + gb_ref[...]
    log_a = -_softplus(-logits) * (1.0 / GLA_TAU)
    hi = log_a.astype(BF16)
    rest = log_a - hi.astype(F32)
    mid = rest.astype(BF16)
    lo = (rest - mid.astype(F32)).astype(BF16)
    tri = tri_ref[...]
    cum_s[...] = (jnp.dot(tri, hi, preferred_element_type=F32) + jnp.dot(tri, mid, preferred_element_type=F32)
                  + jnp.dot(tri, lo, preferred_element_type=F32))
    scale = GLA_DK ** -0.5
    contract_last = (((1,), (1,)), ((), ()))
    contract_first = (((0,), (0,)), ((), ()))
    heads = [(slice(h * GLA_DK, (h + 1) * GLA_DK), slice(h * GLA_DV, (h + 1) * GLA_DV)) for h in range(GLA_HEADS)]

    def finish(o, r0, rows, vs):
        o = o * lax.rsqrt(jnp.mean(o * o, axis=-1, keepdims=True) + NORM_EPS)
        rr = r_ref[pl.ds(r0, rows), vs]
        y_ref[pl.ds(r0, rows), vs] = (o * (gain_ref[:, vs] * rr * _sigmoid(rr))).astype(y_ref.dtype)

    def advance_state(h, kh, ch, vh, rows):
        last = ch[rows - 1:rows, :]
        khat = kh * jnp.exp(last - ch)
        st_ref[h] = st_ref[h] * jnp.exp(last) + lax.dot_general(
            vh.astype(BF16), khat.astype(BF16), contract_first, preferred_element_type=F32)

    def fast(r0):
        t_i = lax.broadcasted_iota(jnp.int32, (chunk, chunk), 0)
        s_i = lax.broadcasted_iota(jnp.int32, (chunk, chunk), 1)
        for h, (ks, vs) in enumerate(heads):
            qh = q_ref[pl.ds(r0, chunk), ks] * scale
            kh = k_ref[pl.ds(r0, chunk), ks]
            vh = v_ref[pl.ds(r0, chunk), vs]
            ch = cum_s[pl.ds(r0, chunk), ks]
            qd = (qh * jnp.exp(ch)).astype(BF16)
            kd = (kh * jnp.exp(-ch)).astype(BF16)
            sc = lax.dot_general(qd, kd, contract_last, preferred_element_type=F32)
            sc = jnp.where(s_i <= t_i, sc, 0.0).astype(BF16)
            o = jnp.dot(sc, vh.astype(BF16), preferred_element_type=F32) + lax.dot_general(
                qd, st_ref[h].astype(BF16), contract_last, preferred_element_type=F32)
            advance_state(h, kh, ch, vh, chunk)
            finish(o, r0, chunk, vs)

    def exact_block(r0, first):
        base = jnp.where(first, 0.0, cum_s[pl.ds(jnp.maximum(r0 - 1, 0), 1), :])
        t_i = lax.broadcasted_iota(jnp.int32, (sub, sub, 1), 0)
        s_i = lax.broadcasted_iota(jnp.int32, (sub, sub, 1), 1)
        for h, (ks, vs) in enumerate(heads):
            qh = q_ref[pl.ds(r0, sub), ks] * scale
            kh = k_ref[pl.ds(r0, sub), ks]
            vh = v_ref[pl.ds(r0, sub), vs]
            ch = cum_s[pl.ds(r0, sub), ks] - base[:, ks]
            diff = ch[:, None, :] - ch[None, :, :]
            dec = jnp.where(s_i <= t_i, jnp.exp(jnp.minimum(diff, 0.0)), 0.0)
            sc = jnp.sum(qh[:, None, :] * kh[None, :, :] * dec, axis=-1, keepdims=True)
            o = jnp.sum(sc * vh[None, :, :], axis=1) + lax.dot_general(
                (qh * jnp.exp(ch)).astype(BF16), st_ref[h].astype(BF16), contract_last, preferred_element_type=F32)
            advance_state(h, kh, ch, vh, sub)
            finish(o, r0, sub, vs)

    def per_chunk(c, carry):
        r0 = pl.multiple_of(c * chunk, sub)
        mild = jnp.min(cum_s[pl.ds(r0 + chunk - 1, 1), :]) >= -GLA_FAST_LIMIT

        @pl.when(mild)
        def _():
            fast(r0)

        @pl.when(jnp.logical_not(mild))
        def _():
            def block(i, carry2):
                exact_block(pl.multiple_of(r0 + i * sub, sub), i == 0)
                return carry2

            lax.fori_loop(0, chunk // sub, block, 0)

        return carry

    lax.fori_loop(0, tt // chunk, per_chunk, 0)


def _gla(q, k, v, r, gd, bsz, t, tt, g_up, g_b, head_gain):
    nt = t // tt
    kw, vw = GLA_KEY_WIDTH, GLA_VAL_WIDTH
    chunk = _pick_tile(tt, 144, GLA_SUB)
    gup = jnp.zeros((LANES, kw), F32).at[:GLA_LORA].set(g_up)
    i = jnp.arange(tt)
    tri = ((i[None, :] <= i[:, None]) & (i[None, :] // chunk == i[:, None] // chunk)).astype(BF16)
    spec = lambda w: pl.BlockSpec((tt, w), lambda b, j: (b * nt + j, 0))
    return pl.pallas_call(
        functools.partial(_gla_kernel, chunk=chunk),
        grid=(bsz, nt),
        in_specs=[spec(kw), spec(kw), spec(vw), spec(vw), spec(LANES), _const_spec((LANES, kw)),
                  _const_spec((1, kw)), _const_spec((1, vw)), _const_spec((tt, tt))],
        out_specs=spec(vw),
        out_shape=jax.ShapeDtypeStruct((bsz * t, vw), BF16),
        scratch_shapes=[pltpu.VMEM((GLA_HEADS, GLA_DV, GLA_DK), F32), pltpu.VMEM((tt, kw), F32)],
        compiler_params=_params("parallel", "arbitrary"),
        name="gla",
    )(q, k, v, r, gd, gup.astype(BF16), g_b.reshape(1, -1), head_gain.reshape(1, -1), tri)


ROUTE_E1, ROUTE_E2, ROUTE_RANK1, ROUTE_RANK2, ROUTE_W1, ROUTE_W2 = range(6)


def _rms(x, gain):
    return x * lax.rsqrt(jnp.mean(x * x, axis=-1, keepdims=True) + NORM_EPS) * gain


def _moe_route_kernel(h_ref, gain_ref, wr_ref, br_ref, tri_ref, info_ref, cnt_ref, carry_s):
    @pl.when(pl.program_id(0) == 0)
    def _():
        carry_s[...] = jnp.zeros_like(carry_s)

    lane = lax.broadcasted_iota(jnp.int32, (1, LANES), 1)
    xn = _rms(h_ref[...], gain_ref[...])
    logits = jnp.dot(xn, wr_ref[...], preferred_element_type=F32, precision=lax.Precision.HIGHEST) + br_ref[...]
    neg = -jnp.inf
    far = 4 * LANES
    first = lambda hit: jnp.min(jnp.where(hit, lane, far), axis=-1, keepdims=True)
    is_group = lane < N_GROUPS
    gl = jnp.where(is_group, logits, neg)
    gmax = jnp.max(gl, axis=-1, keepdims=True)
    g_idx = first(gl == gmax)
    g_prob = 1.0 / jnp.sum(jnp.where(is_group, jnp.exp(logits - gmax), 0.0), axis=-1, keepdims=True)
    in_group = (lane >= N_GROUPS) & (lane < N_GROUPS + N_EXPERTS) & (((lane - N_GROUPS) >> 3) == g_idx)
    el = jnp.where(in_group, logits, neg)
    v1 = jnp.max(el, axis=-1, keepdims=True)
    i1 = first(el == v1)
    el2 = jnp.where(lane == i1, neg, el)
    v2 = jnp.max(el2, axis=-1, keepdims=True)
    i2 = first(el2 == v2)
    e21 = jnp.exp(v2 - v1)
    w1 = g_prob / (1.0 + e21)
    hit1 = lane == i1
    hit2 = lane == i2
    m = jnp.where(hit1 | hit2, 1.0, 0.0)
    before = jnp.dot(tri_ref[...], m.astype(BF16), preferred_element_type=F32) + carry_s[...]
    rank1 = jnp.sum(jnp.where(hit1, before, 0.0), axis=-1, keepdims=True)
    rank2 = jnp.sum(jnp.where(hit2, before, 0.0), axis=-1, keepdims=True)
    carry_s[...] += jnp.sum(m, axis=0, keepdims=True)
    cnt_ref[...] = carry_s[...]
    fields = ((i1 - N_GROUPS).astype(F32), (i2 - N_GROUPS).astype(F32), rank1, rank2, w1, w1 * e21)
    info = jnp.zeros(info_ref.shape, F32)
    for pos, val in enumerate(fields):
        info = jnp.where(lane == pos, val, info)
    info_ref[...] = info


def _row_copy(src, src_row, dst, dst_row, sem):
    return pltpu.make_async_copy(src.at[pl.ds(src_row, 1)], dst.at[pl.ds(dst_row, 1)], sem)


def _start_rows(copies, n):
    def start(j, c):
        for cp in copies(j):
            cp.start()
        return c

    lax.fori_loop(0, n, start, 0, unroll=DMA_UNROLL)


def _wait_rows(copies, n):
    def wait(j, c):
        for cp in copies(j):
            cp.wait()
        return c

    lax.fori_loop(0, n, wait, 0, unroll=DMA_UNROLL)


def _moe_pos_kernel(info_ref, offs_ref, o_ref):
    info = info_ref[...]
    lane = lax.broadcasted_iota(jnp.int32, (1, LANES), 1)
    out = info
    for e_lane, rank_lane in ((ROUTE_E1, ROUTE_RANK1), (ROUTE_E2, ROUTE_RANK2)):
        e = info[:, e_lane:e_lane + 1].astype(jnp.int32)
        off = jnp.sum(jnp.where(lane == e, offs_ref[...], 0.0), axis=-1, keepdims=True)
        out = jnp.where(lane == rank_lane, info[:, rank_lane:rank_lane + 1] + off, out)
    o_ref[...] = out


def _moe_dispatch_kernel(pos1_ref, pos2_ref, h_ref, init_hbm, xs_hbm, sem):
    del init_hbm
    copies = lambda j: (_row_copy(h_ref, j, xs_hbm, pos1_ref[0, 0, j], sem.at[0]),
                        _row_copy(h_ref, j, xs_hbm, pos2_ref[0, 0, j], sem.at[1]))
    _start_rows(copies, h_ref.shape[0])
    _wait_rows(copies, h_ref.shape[0])


def _moe_ffn_kernel(te_ref, xs_ref, gain_ref, wg_ref, wu_ref, wd_ref, ys_ref):
    del te_ref
    xn = _rms(xs_ref[...], gain_ref[...]).astype(BF16)
    gate = jnp.dot(xn, wg_ref[0], preferred_element_type=F32)
    up = jnp.dot(xn, wu_ref[0], preferred_element_type=F32)
    hid = gate * _sigmoid(gate) * up
    ys_ref[...] = jnp.dot(hid.astype(BF16), wd_ref[0], preferred_element_type=F32)


def _moe_combine_kernel(pos1_ref, pos2_ref, nxt1_ref, nxt2_ref, h_ref, info_ref, gain_ref, ys_hbm, o_ref, buf1, buf2,
                        sem, *, final_norm):
    i = pl.program_id(0)
    tm = h_ref.shape[0]
    slot = i % 2

    def copies(p1_ref, p2_ref, sl):
        return lambda j: (_row_copy(ys_hbm, p1_ref[0, 0, j], buf1.at[sl], j, sem.at[sl, 0]),
                          _row_copy(ys_hbm, p2_ref[0, 0, j], buf2.at[sl], j, sem.at[sl, 1]))

    @pl.when(i == 0)
    def _():
        _start_rows(copies(pos1_ref, pos2_ref, 0), tm)

    @pl.when(i + 1 < pl.num_programs(0))
    def _():
        _start_rows(copies(nxt1_ref, nxt2_ref, 1 - slot), tm)

    _wait_rows(copies(pos1_ref, pos2_ref, slot), tm)
    info = info_ref[...]
    w1 = info[:, ROUTE_W1:ROUTE_W1 + 1]
    w2 = info[:, ROUTE_W2:ROUTE_W2 + 1]
    out = h_ref[...] + w1 * buf1[slot] + w2 * buf2[slot]
    o_ref[...] = _rms(out, gain_ref[...]) if final_norm else out


def _moe(h, gain, w_group, b_group, w_router, b_router, w_gate, w_up, w_down, tm, tg, final_gain=None):
    n, d = h.shape
    nt = n // tm
    wr = jnp.zeros((d, LANES), F32).at[:, :N_GROUPS].set(w_group).at[:, N_GROUPS:N_GROUPS + N_EXPERTS].set(w_router)
    br = jnp.zeros((1, LANES), F32).at[0, :N_GROUPS].set(b_group).at[0, N_GROUPS:N_GROUPS + N_EXPERTS].set(b_router)
    i = jnp.arange(tm)
    tri = (i[None, :] < i[:, None]).astype(BF16)
    info, cnt = pl.pallas_call(
        _moe_route_kernel,
        grid=(nt,),
        in_specs=[pl.BlockSpec((tm, d), lambda i: (i, 0)), _const_spec((1, d)), _const_spec((d, LANES)),
                  _const_spec((1, LANES)), _const_spec((tm, tm))],
        out_specs=[pl.BlockSpec((tm, LANES), lambda i: (i, 0)), _const_spec((1, LANES))],
        out_shape=[jax.ShapeDtypeStruct((n, LANES), F32), jax.ShapeDtypeStruct((1, LANES), F32)],
        scratch_shapes=[pltpu.VMEM((1, LANES), F32)],
        compiler_params=_params("arbitrary"),
        name="moe_route",
    )(h, gain.reshape(1, d), wr, br, tri)

    counts = cnt[:, N_GROUPS:N_GROUPS + N_EXPERTS].astype(jnp.int32)
    padded = (counts + tg - 1) // tg * tg
    ends = jnp.cumsum(padded, axis=1)
    offs = jnp.zeros((1, LANES), F32).at[:, :N_EXPERTS].set((ends - padded).astype(F32))
    n_rows = (2 * n + tg - 1) // tg * tg + N_EXPERTS * tg
    n_tiles = n_rows // tg
    tile_expert = jnp.minimum(jnp.sum(jnp.arange(n_tiles)[:, None] * tg >= ends, axis=1),
                              N_EXPERTS - 1).astype(jnp.int32)
    info = pl.pallas_call(
        _moe_pos_kernel,
        grid=(nt,),
        in_specs=[pl.BlockSpec((tm, LANES), lambda i: (i, 0)), _const_spec((1, LANES))],
        out_specs=pl.BlockSpec((tm, LANES), lambda i: (i, 0)),
        out_shape=jax.ShapeDtypeStruct((n, LANES), F32),
        compiler_params=_params("parallel"),
        name="moe_pos",
    )(info, offs)
    pos1 = info[:, ROUTE_RANK1].astype(jnp.int32).reshape(nt, 1, tm)
    pos2 = info[:, ROUTE_RANK2].astype(jnp.int32).reshape(nt, 1, tm)

    pos_spec = pl.BlockSpec((1, 1, tm), lambda i: (i, 0, 0), memory_space=pltpu.SMEM)
    next_spec = pl.BlockSpec((1, 1, tm), lambda i: (jnp.minimum(i + 1, nt - 1), 0, 0), memory_space=pltpu.SMEM)
    any_spec = pl.BlockSpec(memory_space=pl.ANY)
    xs = pl.pallas_call(
        _moe_dispatch_kernel,
        grid=(nt,),
        in_specs=[pos_spec, pos_spec, pl.BlockSpec((tm, d), lambda i: (i, 0)), any_spec],
        out_specs=any_spec,
        out_shape=jax.ShapeDtypeStruct((n_rows, d), F32),
        scratch_shapes=[pltpu.SemaphoreType.DMA((2,))],
        input_output_aliases={3: 0},
        compiler_params=_params("arbitrary"),
        name="moe_dispatch",
    )(pos1, pos2, h, jnp.zeros((n_rows, d), F32))

    ys = pl.pallas_call(
        _moe_ffn_kernel,
        grid_spec=pltpu.PrefetchScalarGridSpec(
            num_scalar_prefetch=1,
            grid=(n_tiles,),
            in_specs=[pl.BlockSpec((tg, d), lambda i, te: (i, 0)),
                      pl.BlockSpec((1, d), lambda i, te: (0, 0)),
                      pl.BlockSpec((1, d, EXPERT_FF), lambda i, te: (te[i], 0, 0)),
                      pl.BlockSpec((1, d, EXPERT_FF), lambda i, te: (te[i], 0, 0)),
                      pl.BlockSpec((1, EXPERT_FF, d), lambda i, te: (te[i], 0, 0))],
            out_specs=pl.BlockSpec((tg, d), lambda i, te: (i, 0))),
        out_shape=jax.ShapeDtypeStruct((n_rows, d), F32),
        compiler_params=_params("arbitrary"),
        name="moe_ffn",
    )(tile_expert, xs, gain.reshape(1, d), w_gate.astype(BF16), w_up.astype(BF16), w_down.astype(BF16))

    final = final_gain is not None
    return pl.pallas_call(
        functools.partial(_moe_combine_kernel, final_norm=final),
        grid=(nt,),
        in_specs=[pos_spec, pos_spec, next_spec, next_spec, pl.BlockSpec((tm, d), lambda i: (i, 0)),
                  pl.BlockSpec((tm, LANES), lambda i: (i, 0)), _const_spec((1, d)), any_spec],
        out_specs=pl.BlockSpec((tm, d), lambda i: (i, 0)),
        out_shape=jax.ShapeDtypeStruct((n, d), F32),
        scratch_shapes=[pltpu.VMEM((2, tm, d), F32), pltpu.VMEM((2, tm, d), F32), pltpu.SemaphoreType.DMA((2, 2))],
        compiler_params=_params("arbitrary"),
        name="moe_combine",
    )(pos1, pos2, pos1, pos2, h, info, (final_gain if final else gain).reshape(1, d), ys)


def _even_mixer(h, bsz, t, tm, tt, norm, w_in, mu, decay0, decay_up, iclr0, iclr_up, gate_up, k_k, k_a, r_k,
                ln_g, ln_b, conv_w, conv_b, lru_wa, lru_ba, lru_wx, lru_bx, lru_lambda, w_out):
    p, xb, gb = _norm_proj(h, norm, w_in, (RWKV_IN, LRU_WIDTH, LRU_WIDTH), tm)
    r, w, k, v, kk, b, g, bonus = _rwkv_prep(p, bsz, t, tt, mu, decay0, decay_up, iclr0, iclr_up, gate_up, k_k,
                                             k_a, r_k)
    o = _rwkv_chunked(r, w, k, v, kk, b, bsz, t, tt)
    y_a = _rwkv_post(o, bonus, g, ln_g, ln_b, bsz, t, tt)
    y_b = _lru(xb, gb, bsz, t, tt, conv_w, conv_b, lru_wa, lru_ba, lru_wx, lru_bx, lru_lambda)
    return _out_proj(h, (y_a, y_b), (w_out[:RWKV_WIDTH], w_out[RWKV_WIDTH:]), tm)


def _odd_mixer(h, bsz, t, tm, tt, norm, w_in, g_up, g_b, head_gain, w_out):
    kw, vw = GLA_KEY_WIDTH, GLA_VAL_WIDTH
    d = w_in.shape[0]
    w_packed = jnp.concatenate(
        [w_in[:, :2 * kw + vw], w_in[:, 2 * kw + vw + GLA_LORA:], w_in[:, 2 * kw + vw:2 * kw + vw + GLA_LORA],
         jnp.zeros((d, LANES - GLA_LORA), w_in.dtype)], axis=1)
    q, k, v, r, gd = _norm_proj(h, norm, w_packed, (kw, kw, vw, vw, LANES), tm)
    y = _gla(q, k, v, r, gd, bsz, t, tt, g_up, g_b, head_gain)
    return _out_proj(h, (y,), (w_out,), tm)


def kernel(x, meta, e_norm, e_w_in, e_mu, e_decay0, e_decay_up, e_iclr0, e_iclr_up, e_gate_up, e_k_k, e_k_a, e_r_k, e_ln_g, e_ln_b, e_conv_w, e_conv_b, e_lru_wa, e_lru_ba, e_lru_wx, e_lru_bx, e_lru_lambda, e_w_out, o_norm, o_w_in, o_g_up, o_g_b, o_head_gain, o_w_out, m_norm, m_w_group, m_b_group, m_w_router, m_b_router, m_w_gate, m_w_up, m_w_down, final_norm):
    bsz, seq, d = x.shape
    t = seq + N_META
    n = bsz * t
    tm = _pick_tile(n, 384)
    tt = _pick_tile(t, 432)
    depth = m_norm.shape[0]
    h = jnp.concatenate([jnp.broadcast_to(meta[None].astype(x.dtype), (bsz, N_META, d)), x], axis=1).reshape(n, d)
    for layer in range(depth):
        i = layer // 2
        if layer % 2 == 0:
            h = _even_mixer(h, bsz, t, tm, tt, e_norm[i], e_w_in[i], e_mu[i], e_decay0[i], e_decay_up[i],
                            e_iclr0[i], e_iclr_up[i], e_gate_up[i], e_k_k[i], e_k_a[i], e_r_k[i], e_ln_g[i],
                            e_ln_b[i], e_conv_w[i], e_conv_b[i], e_lru_wa[i], e_lru_ba[i], e_lru_wx[i],
                            e_lru_bx[i], e_lru_lambda[i], e_w_out[i])
        else:
            h = _odd_mixer(h, bsz, t, tm, tt, o_norm[i], o_w_in[i], o_g_up[i], o_g_b[i], o_head_gain[i], o_w_out[i])
        h = _moe(h, m_norm[layer], m_w_group[layer], m_b_group[layer], m_w_router[layer], m_b_router[layer],
                 m_w_gate[layer], m_w_up[layer], m_w_down[layer], tm, MOE_TILE,
                 final_gain=final_norm if layer == depth - 1 else None)
    return h.reshape(bsz, t, d)[:, N_META:]
```
